```python
import jax, jax.numpy as jnp
from jax import lax
import numpy as np

D_MODEL = 1024
BATCH = 8
SEQ = 4096
DEPTH = 2

N_MEM = 256
POOL_WINDOWS = (2, 4, 8, 16)
POOL_WIDTH = 512
POOL_GROUP = POOL_WIDTH // len(POOL_WINDOWS)
FOX_HEADS = 8
FOX_HEAD_DIM = 64
FOX_WIDTH = FOX_HEADS * FOX_HEAD_DIM
MEM_HEADS = 4
MEM_HEAD_DIM = 128
MEM_WIDTH = MEM_HEADS * MEM_HEAD_DIM
N_BRANCH = 3
Q_BLOCK = 128
D_FF = 2816
N_EXPERTS = 8
TOP_K = 2
D_EXPERT = 3584
N_DENSE = (DEPTH + 1) // 2
N_MOE = DEPTH // 2
EPS = 1e-6

OFF_POOL = 0
OFF_Q = OFF_POOL + POOL_WIDTH
OFF_K = OFF_Q + FOX_WIDTH
OFF_V = OFF_K + FOX_WIDTH
OFF_F = OFF_V + FOX_WIDTH
OFF_MQ = OFF_F + FOX_HEADS
OFF_G = OFF_MQ + MEM_WIDTH
N_IN = OFF_G + N_BRANCH * D_MODEL

kernel_name = "hybrid_pool_fox_memxattn_moe_block"


def rms_norm(x, g):
    xf = x.astype(jnp.float32)
    y = xf * lax.rsqrt(jnp.mean(xf * xf, axis=-1, keepdims=True) + EPS)
    return (y * g.astype(jnp.float32)).astype(x.dtype)


def pool_mixer(u, pool_w, pool_scale):
    B, S, _ = u.shape
    G = len(POOL_WINDOWS)
    uf = u.astype(jnp.float32).reshape(B, S, G, POOL_GROUP)
    cpad = jnp.pad(jnp.cumsum(uf, axis=1), ((0, 0), (1, 0), (0, 0), (0, 0)))
    pos = jnp.arange(S)
    outs = []
    for g, w in enumerate(POOL_WINDOWS):
        c = cpad[:, :, g]
        lag = jnp.pad(c, ((0, 0), (w, 0), (0, 0)))[:, 1:S + 1]
        cnt = jnp.minimum(pos + 1, w).astype(jnp.float32)[None, :, None]
        outs.append((c[:, 1:] - lag) / cnt - uf[:, :, g])
    d = jnp.stack(outs, axis=2)
    y = jnp.einsum('bsgc,gcd->bsgd', d, pool_w.astype(jnp.float32))
    return (y.reshape(B, S, POOL_WIDTH) * pool_scale.astype(jnp.float32)).astype(u.dtype)


def head_rms(x, g):
    return rms_norm(x, g)


def fox_attention(q, k, v, f_logit, q_g, k_g):
    B, S, _ = q.shape
    H, dh = FOX_HEADS, FOX_HEAD_DIM
    q = head_rms(q.reshape(B, S, H, dh), q_g)
    k = head_rms(k.reshape(B, S, H, dh), k_g)
    v = v.reshape(B, S, H, dh)
    logf = jax.nn.log_sigmoid(f_logit.astype(jnp.float32))
    c = jnp.cumsum(logf, axis=1).transpose(0, 2, 1)
    nb = S // Q_BLOCK
    qb = q.reshape(B, nb, Q_BLOCK, H, dh).transpose(1, 0, 2, 3, 4)
    cb = c.reshape(B, H, nb, Q_BLOCK).transpose(2, 0, 1, 3)
    kpos = jnp.arange(S)
    scale = dh ** -0.5

    def block(args):
        qi, ci, i = args
        s = jnp.einsum('bqhd,bkhd->bhqk', qi, k, preferred_element_type=jnp.float32) * scale
        s = s + (ci[..., :, None] - c[..., None, :])
        qpos = i * Q_BLOCK + jnp.arange(Q_BLOCK)
        s = jnp.where(kpos[None, :] <= qpos[:, None], s, -jnp.inf)
        p = jax.nn.softmax(s, axis=-1)
        return jnp.einsum('bhqk,bkhd->bqhd', p.astype(v.dtype), v)

    o = lax.map(block, (qb, cb, jnp.arange(nb)))
    return o.transpose(1, 0, 2, 3, 4).reshape(B, S, FOX_WIDTH)


def mem_attention(qm, mem_n, w_kv, q_g, k_g):
    B, S, _ = qm.shape
    M = mem_n.shape[1]
    H, dh = MEM_HEADS, MEM_HEAD_DIM
    kv = mem_n @ w_kv
    k = head_rms(kv[..., :MEM_WIDTH].reshape(B, M, H, dh), k_g)
    v = kv[..., MEM_WIDTH:].reshape(B, M, H, dh)
    q = head_rms(qm.reshape(B, S, H, dh), q_g)
    s = jnp.einsum('bshd,bmhd->bhsm', q, k, preferred_element_type=jnp.float32) * (dh ** -0.5)
    p = jax.nn.softmax(s, axis=-1)
    o = jnp.einsum('bhsm,bmhd->bshd', p.astype(v.dtype), v)
    return o.reshape(B, S, MEM_WIDTH)


def swiglu(h, w_gu, w_d):
    gu = h @ w_gu
    f = w_d.shape[0]
    return (jax.nn.silu(gu[..., :f]) * gu[..., f:]) @ w_d


def moe_ffn(h, w_r, b_r, w_gu, w_d):
    B, S, D = h.shape
    t = h.reshape(B * S, D)
    logits = (t @ w_r).astype(jnp.float32) + b_r.astype(jnp.float32)
    top_v, top_i = lax.top_k(logits, TOP_K)
    wts = jax.nn.softmax(top_v, axis=-1)
    combine = jnp.sum(jax.nn.one_hot(top_i, N_EXPERTS, dtype=jnp.float32) * wts[..., None], axis=1)
    y = jnp.zeros((B * S, D), jnp.float32)
    for e in range(N_EXPERTS):
        y = y + combine[:, e:e + 1] * swiglu(t, w_gu[e], w_d[e]).astype(jnp.float32)
    return y.reshape(B, S, D).astype(h.dtype)


def setup_inputs(seed: int = 0) -> dict:
    key = jax.random.key(seed)
    ks = jax.random.split(key, 24)
    D = D_MODEL
    nrm = lambda k, shape, fan: jax.random.normal(k, shape, jnp.float32) * (fan ** -0.5)
    gain = lambda k, shape: 1.0 + 0.02 * jax.random.normal(k, shape, jnp.float32)
    return {
        "x": jax.random.normal(ks[0], (BATCH, SEQ, D), jnp.float32),
        "mem": jax.random.normal(ks[1], (BATCH, N_MEM, D), jnp.float32),
        "mix_norm_g": gain(ks[2], (DEPTH, D)),
        "w_in": nrm(ks[3], (DEPTH, D, N_IN), D),
        "b_forget": jax.random.uniform(ks[4], (DEPTH, FOX_HEADS), jnp.float32, 1.0, 5.0),
        "fox_q_g": gain(ks[5], (DEPTH, FOX_HEAD_DIM)),
        "fox_k_g": gain(ks[6], (DEPTH, FOX_HEAD_DIM)),
        "pool_w": nrm(ks[7], (DEPTH, len(POOL_WINDOWS), POOL_GROUP, POOL_GROUP), POOL_GROUP),
        "pool_scale": gain(ks[8], (DEPTH, POOL_WIDTH)),
        "mem_norm_g": gain(ks[9], (DEPTH, D)),
        "w_mem_kv": nrm(ks[10], (DEPTH, D, 2 * MEM_WIDTH), D),
        "mem_q_g": gain(ks[11], (DEPTH, MEM_HEAD_DIM)),
        "mem_k_g": gain(ks[12], (DEPTH, MEM_HEAD_DIM)),
        "w_pool_br": nrm(ks[13], (DEPTH, POOL_WIDTH, D), POOL_WIDTH),
        "w_fox_br": nrm(ks[14], (DEPTH, FOX_WIDTH, D), FOX_WIDTH),
        "w_mem_br": nrm(ks[15], (DEPTH, MEM_WIDTH, D), MEM_WIDTH),
        "w_out": 0.5 * nrm(ks[16], (DEPTH, D, D), D),
        "ffn_norm_g": gain(ks[17], (DEPTH, D)),
        "w_ffn_gu": nrm(ks[18], (N_DENSE, D, 2 * D_FF), D),
        "w_ffn_down": 0.5 * nrm(ks[19], (N_DENSE, D_FF, D), D_FF),
        "w_router": nrm(ks[20], (N_MOE, D, N_EXPERTS), D),
        "b_router": 0.01 * jax.random.normal(ks[21], (N_MOE, N_EXPERTS), jnp.float32),
        "w_exp_gu": nrm(ks[22], (N_MOE, N_EXPERTS, D, 2 * D_EXPERT), D),
        "w_exp_down": 0.5 * nrm(ks[23], (N_MOE, N_EXPERTS, D_EXPERT, D), D_EXPERT),
    }


def reference(x, mem, mix_norm_g, w_in, b_forget, fox_q_g, fox_k_g, pool_w, pool_scale,
              mem_norm_g, w_mem_kv, mem_q_g, mem_k_g, w_pool_br, w_fox_br, w_mem_br, w_out,
              ffn_norm_g, w_ffn_gu, w_ffn_down, w_router, b_router, w_exp_gu, w_exp_down):
    B, S, D = x.shape
    for layer in range(DEPTH):
        h = rms_norm(x, mix_norm_g[layer])
        z = h @ w_in[layer]
        pool_o = pool_mixer(z[..., OFF_POOL:OFF_Q], pool_w[layer], pool_scale[layer])
        fox_o = fox_attention(z[..., OFF_Q:OFF_K], z[..., OFF_K:OFF_V], z[..., OFF_V:OFF_F],
                              z[..., OFF_F:OFF_MQ] + b_forget[layer].astype(z.dtype),
                              fox_q_g[layer], fox_k_g[layer])
        mem_n = rms_norm(mem, mem_norm_g[layer])
        mem_o = mem_attention(z[..., OFF_MQ:OFF_G], mem_n, w_mem_kv[layer],
                              mem_q_g[layer], mem_k_g[layer])
        gates = jax.nn.sigmoid(z[..., OFF_G:].astype(jnp.float32)).reshape(B, S, N_BRANCH, D)
        merged = (gates[:, :, 0] * (pool_o @ w_pool_br[layer]).astype(jnp.float32)
                  + gates[:, :, 1] * (fox_o @ w_fox_br[layer]).astype(jnp.float32)
                  + gates[:, :, 2] * (mem_o @ w_mem_br[layer]).astype(jnp.float32))
        x = x + merged.astype(x.dtype) @ w_out[layer]
        h = rms_norm(x, ffn_norm_g[layer])
        if layer % 2 == 0:
            y = swiglu(h, w_ffn_gu[layer // 2], w_ffn_down[layer // 2])
        else:
            y = moe_ffn(h, w_router[layer // 2], b_router[layer // 2],
                        w_exp_gu[layer // 2], w_exp_down[layer // 2])
        x = x + y.astype(x.dtype)
    return x
```

```python
import functools

import jax
import jax.numpy as jnp
from jax import lax
from jax.experimental import pallas as pl
from jax.experimental.pallas import tpu as pltpu

D_MODEL = 1024
N_MEM = 256
POOL_WINDOWS = (2, 4, 8, 16)
POOL_WIDTH = 512
POOL_GROUP = 128
POOL_HALO = 16
FOX_HEADS = 8
FOX_HEAD_DIM = 64
FOX_WIDTH = 512
MEM_HEADS = 4
MEM_HEAD_DIM = 128
MEM_WIDTH = 512
N_BRANCH = 3
D_FF = 2816
N_EXPERTS = 8
TOP_K = 2
D_EXPERT = 3584
EPS = 1e-6

OFF_POOL = 0
OFF_Q = OFF_POOL + POOL_WIDTH
OFF_K = OFF_Q + FOX_WIDTH
OFF_V = OFF_K + FOX_WIDTH
OFF_F = OFF_V + FOX_WIDTH
OFF_MQ = OFF_F + FOX_HEADS
OFF_G = OFF_MQ + MEM_WIDTH

LANES = 128
BF = jnp.bfloat16
F32 = jnp.float32

TM_IN = 512
TQ = 512
TM_MERGE = 256
TM_EXPERT = 256
GATHER_ROWS = 256
FF_CHUNK = 256
VMEM_LIMIT = 58 * 1024 * 1024


def _resident(shape):
    nd = len(shape)
    return pl.BlockSpec(shape, lambda *_: (0,) * nd, pipeline_mode=pl.Buffered(1))


def _params(*sem):
    return pltpu.CompilerParams(dimension_semantics=sem, vmem_limit_bytes=VMEM_LIMIT)


def _dot(a, b):
    return jnp.dot(a, b, preferred_element_type=F32)


def _rms(x, g):
    ms = jnp.mean(x * x, axis=-1, keepdims=True)
    return x * lax.rsqrt(ms + EPS) * g


def _sigmoid(z):
    return 1.0 / (1.0 + jnp.exp(-z))


def _mem_kv_kernel(mem_ref, g_ref, w_ref, kg_ref, kT_ref, v_ref):
    mn = _rms(mem_ref[...], g_ref[...]).astype(BF)
    kv = _dot(mn, w_ref[...])
    for h in range(MEM_HEADS):
        kh = kv[:, h * MEM_HEAD_DIM:(h + 1) * MEM_HEAD_DIM]
        kT_ref[h] = _rms(kh, kg_ref[...]).T.astype(BF)
    v_ref[...] = kv[:, MEM_WIDTH:].astype(BF)


def _mem_kv(mem, g, w_kv, k_g):
    B, M, D = mem.shape
    return pl.pallas_call(
        _mem_kv_kernel,
        grid=(B,),
        in_specs=[pl.BlockSpec((None, M, D), lambda b: (b, 0, 0)),
                  _resident((1, D)), _resident((D, 2 * MEM_WIDTH)), _resident((1, MEM_HEAD_DIM))],
        out_specs=[pl.BlockSpec((None, MEM_HEADS, MEM_HEAD_DIM, M), lambda b: (b, 0, 0, 0)),
                   pl.BlockSpec((None, M, MEM_WIDTH), lambda b: (b, 0, 0))],
        out_shape=[jax.ShapeDtypeStruct((B, MEM_HEADS, MEM_HEAD_DIM, M), BF),
                   jax.ShapeDtypeStruct((B, M, MEM_WIDTH), BF)],
        compiler_params=_params("arbitrary"),
        name="mem_kv",
    )(mem, g, w_kv, k_g)


def _in_proj_kernel(x_ref, g_ref, wa_ref, wf_ref, bf_ref, qg_ref, kg_ref, bd_ref, mqg_ref,
                    mkT_ref, mv_ref, u_ref, q_ref, kT_ref, v_ref, lf_ref, mo_ref):
    tm = x_ref.shape[0]
    hb = _rms(x_ref[...], g_ref[...]).astype(BF)

    u_ref[...] = _dot(hb, wa_ref[:, OFF_POOL:OFF_Q]).astype(BF)

    bd = bd_ref[...]
    zq = _dot(hb, wa_ref[:, OFF_Q:OFF_K])
    qms = _dot((zq * zq).astype(BF), bd)
    q_ref[...] = (zq * lax.rsqrt(qms + EPS) * qg_ref[...]).astype(BF)
    zk = _dot(hb, wa_ref[:, OFF_K:OFF_V])
    kms = _dot((zk * zk).astype(BF), bd)
    kn = zk * lax.rsqrt(kms + EPS) * kg_ref[...]
    kT_ref[...] = kn.T.reshape(FOX_HEADS // 2, 2 * FOX_HEAD_DIM, tm).astype(BF)
    v_ref[...] = _dot(hb, wa_ref[:, OFF_V:OFF_F]).astype(BF)

    zf = _dot(hb, wf_ref[...]) + bf_ref[...]
    logf = jnp.minimum(zf, 0.0) - jnp.log(1.0 + jnp.exp(-jnp.abs(zf)))
    lf_ref[...] = logf.T[0:FOX_HEADS, :]

    zm = _dot(hb, wa_ref[:, OFF_F:OFF_F + MEM_WIDTH])
    for h in range(MEM_HEADS):
        sl = slice(h * MEM_HEAD_DIM, (h + 1) * MEM_HEAD_DIM)
        qn = _rms(zm[:, sl], mqg_ref[...]).astype(BF)
        s = _dot(qn, mkT_ref[h])
        p = jnp.exp(s - jnp.max(s, axis=-1, keepdims=True))
        l = jnp.sum(p, axis=-1, keepdims=True)
        mo_ref[:, sl] = (_dot(p.astype(BF), mv_ref[:, sl]) / l).astype(BF)


def _in_proj(x2d, B, S, g, wa, wf, bfg, qg, kg, bd, mqg, mkT, mv):
    T, D = x2d.shape
    tm = TM_IN
    nk = S // tm
    wa_cols = wa.shape[1]
    M = mkT.shape[-1]
    tok = lambda i: (i, 0)
    return pl.pallas_call(
        _in_proj_kernel,
        grid=(T // tm,),
        in_specs=[pl.BlockSpec((tm, D), tok),
                  _resident((1, D)), _resident((D, wa_cols)), _resident((D, LANES)),
                  _resident((1, LANES)), _resident((1, FOX_WIDTH)), _resident((1, FOX_WIDTH)),
                  _resident((FOX_WIDTH, FOX_WIDTH)), _resident((1, MEM_HEAD_DIM)),
                  pl.BlockSpec((None, MEM_HEADS, MEM_HEAD_DIM, M), lambda i: (i // nk, 0, 0, 0)),
                  pl.BlockSpec((None, M, MEM_WIDTH), lambda i: (i // nk, 0, 0))],
        out_specs=[pl.BlockSpec((tm, POOL_WIDTH), tok),
                   pl.BlockSpec((tm, FOX_WIDTH), tok),
                   pl.BlockSpec((None, FOX_HEADS // 2, None, 2 * FOX_HEAD_DIM, tm),
                                lambda i: (i // nk, 0, i % nk, 0, 0)),
                   pl.BlockSpec((tm, FOX_WIDTH), tok),
                   pl.BlockSpec((None, FOX_HEADS, tm), lambda i: (i // nk, 0, i % nk)),
                   pl.BlockSpec((tm, MEM_WIDTH), tok)],
        out_shape=[jax.ShapeDtypeStruct((T, POOL_WIDTH), BF),
                   jax.ShapeDtypeStruct((T, FOX_WIDTH), BF),
                   jax.ShapeDtypeStruct((B, FOX_HEADS // 2, nk, 2 * FOX_HEAD_DIM, tm), BF),
                   jax.ShapeDtypeStruct((T, FOX_WIDTH), BF),
                   jax.ShapeDtypeStruct((B, FOX_HEADS, S), F32),
                   jax.ShapeDtypeStruct((T, MEM_WIDTH), BF)],
        compiler_params=_params("arbitrary"),
        name="in_proj",
    )(x2d, g, wa, wf, bfg, qg, kg, bd, mqg, mkT, mv)


def _cumsum_kernel(lf_ref, c_ref):
    x = lf_ref[...]
    S = x.shape[1]
    lane = lax.broadcasted_iota(jnp.int32, x.shape, 1)
    k = 1
    while k < S:
        x = x + jnp.where(lane >= k, pltpu.roll(x, k, axis=1), 0.0)
        k *= 2
    c_ref[...] = x


def _cumsum(logf):
    B, H, S = logf.shape
    spec = pl.BlockSpec((None, H, S), lambda b: (b, 0, 0))
    return pl.pallas_call(
        _cumsum_kernel, grid=(B,), in_specs=[spec], out_specs=spec,
        out_shape=jax.ShapeDtypeStruct((B, H, S), F32),
        compiler_params=_params("arbitrary"), name="forget_cumsum",
    )(logf)


def _fox_kernel(q_ref, kT_ref, v_ref, c_ref, o_ref):
    tq = q_ref.shape[0]
    i = pl.program_id(2)
    q = q_ref[...]
    lane = lax.broadcasted_iota(jnp.int32, q.shape, 1)
    first = lane < FOX_HEAD_DIM
    zero = jnp.zeros_like(q)
    q_heads = (jnp.where(first, q, zero), jnp.where(first, zero, q))
    row = lax.broadcasted_iota(jnp.int32, (tq, tq), 0)
    col = lax.broadcasted_iota(jnp.int32, (tq, tq), 1)
    causal = col <= row

    def step(j, carry, diagonal):
        kT = kT_ref[j]
        v = v_ref[pl.ds(pl.multiple_of(j * tq, tq), tq), :]
        c = c_ref[j]
        out = []
        for h in range(2):
            m, l, acc = carry[h]
            s = _dot(q_heads[h], kT) - c[h:h + 1, :]
            if diagonal:
                s = jnp.where(causal, s, -jnp.inf)
            m_new = jnp.maximum(m, jnp.max(s, axis=-1, keepdims=True))
            p = jnp.exp(s - m_new)
            alpha = jnp.exp(m - m_new)
            l = alpha * l + jnp.sum(p, axis=-1, keepdims=True)
            acc = alpha * acc + _dot(p.astype(BF), v)
            out.append((m_new, l, acc))
        return tuple(out)

    init = tuple((jnp.full((tq, 1), -jnp.inf, F32), jnp.zeros((tq, 1), F32),
                  jnp.zeros((tq, 2 * FOX_HEAD_DIM), F32)) for _ in range(2))
    carry = lax.fori_loop(0, i, lambda j, c: step(j, c, False), init)
    (_, l0, a0), (_, l1, a1) = step(i, carry, True)
    o_ref[...] = jnp.where(first, a0 / l0, a1 / l1).astype(BF)


def _fox(q, kT, v, c):
    B, S, W = q.shape
    nk = S // TQ
    hp = FOX_HEADS // 2
    pair = 2 * FOX_HEAD_DIM
    return pl.pallas_call(
        _fox_kernel,
        grid=(B, hp, nk),
        in_specs=[pl.BlockSpec((None, TQ, pair), lambda b, h, i: (b, i, h)),
                  pl.BlockSpec((None, None, nk, pair, TQ), lambda b, h, i: (b, h, 0, 0, 0)),
                  pl.BlockSpec((None, S, pair), lambda b, h, i: (b, 0, h)),
                  pl.BlockSpec((None, None, nk, 2, TQ), lambda b, h, i: (b, h, 0, 0, 0))],
        out_specs=pl.BlockSpec((None, TQ, pair), lambda b, h, i: (b, i, h)),
        out_shape=jax.ShapeDtypeStruct((B, S, W), BF),
        compiler_params=_params("arbitrary", "arbitrary", "arbitrary"),
        name="fox_attention",
    )(q, kT, v, c)


def _merge_body(n_seq_tiles, x_ref, u_ref, up_ref, fox_ref, mo_ref, g_ref, wg_ref, pw_ref, ps_ref,
                wp_ref, wfb_ref, wmb_ref, wo_ref):
    tm = x_ref.shape[0]
    i = pl.program_id(0)
    x = x_ref[...]
    hb = _rms(x, g_ref[...]).astype(BF)

    seq_tile = i % n_seq_tiles
    halo = jnp.where(seq_tile == 0, 0.0, up_ref[...].astype(F32))
    u = u_ref[...].astype(F32)
    ext = jnp.concatenate([halo, u], axis=0)
    pos = seq_tile * tm + lax.broadcasted_iota(jnp.int32, (tm, 1), 0)
    outs = []
    for g, w in enumerate(POOL_WINDOWS):
        sl = slice(g * POOL_GROUP, (g + 1) * POOL_GROUP)
        s = ext[:, sl]
        k = 1
        while k < w:
            s = s + pltpu.roll(s, k, axis=0)
            k *= 2
        cnt = jnp.minimum(pos + 1, w).astype(F32)
        d = s[POOL_HALO:, :] / cnt - u[:, sl]
        outs.append(_dot(d.astype(BF), pw_ref[g]))
    pool_o = (jnp.concatenate(outs, axis=1) * ps_ref[...]).astype(BF)

    merged = _sigmoid(_dot(hb, wg_ref[:, 0:D_MODEL])) * _dot(pool_o, wp_ref[...])
    merged += _sigmoid(_dot(hb, wg_ref[:, D_MODEL:2 * D_MODEL])) * _dot(fox_ref[...], wfb_ref[...])
    merged += _sigmoid(_dot(hb, wg_ref[:, 2 * D_MODEL:])) * _dot(mo_ref[...], wmb_ref[...])
    return x + _dot(merged.astype(BF), wo_ref[...])


def _swiglu(hb, wgu_ref, wd_ref, act_ref, d_hidden):
    for c in range(d_hidden // FF_CHUNK):
        lo = c * FF_CHUNK
        g = _dot(hb, wgu_ref[:, lo:lo + FF_CHUNK])
        up = _dot(hb, wgu_ref[:, d_hidden + lo:d_hidden + lo + FF_CHUNK])
        act_ref[:, lo:lo + FF_CHUNK] = (g * _sigmoid(g) * up).astype(BF)
    return _dot(act_ref[...], wd_ref[...])


def _merge_ffn_kernel(n_seq_tiles, x_ref, u_ref, up_ref, fox_ref, mo_ref, g_ref, wg_ref, pw_ref,
                      ps_ref, wp_ref, wfb_ref, wmb_ref, wo_ref, g2_ref, wgu_ref, wd_ref,
                      o_ref, act_ref):
    x1 = _merge_body(n_seq_tiles, x_ref, u_ref, up_ref, fox_ref, mo_ref, g_ref, wg_ref, pw_ref,
                     ps_ref, wp_ref, wfb_ref, wmb_ref, wo_ref)
    h2 = _rms(x1, g2_ref[...]).astype(BF)
    o_ref[...] = x1 + _swiglu(h2, wgu_ref, wd_ref, act_ref, D_FF)


def _split_bf16(a):
    hi = a.astype(BF)
    return hi, (a - hi.astype(F32)).astype(BF)


def _merge_router_kernel(n_seq_tiles, x_ref, u_ref, up_ref, fox_ref, mo_ref, g_ref, wg_ref, pw_ref,
                         ps_ref, wp_ref, wfb_ref, wmb_ref, wo_ref, g2_ref, wr_ref, br_ref,
                         x1_ref, h2_ref, route_ref):
    x1 = _merge_body(n_seq_tiles, x_ref, u_ref, up_ref, fox_ref, mo_ref, g_ref, wg_ref, pw_ref,
                     ps_ref, wp_ref, wfb_ref, wmb_ref, wo_ref)
    x1_ref[...] = x1
    h2 = _rms(x1, g2_ref[...])
    h2_ref[...] = h2

    h_hi, h_lo = _split_bf16(h2)
    w_hi, w_lo = _split_bf16(wr_ref[...])
    logits = _dot(h_hi, w_hi) + (_dot(h_hi, w_lo) + _dot(h_lo, w_hi)) + br_ref[...]
    lane = lax.broadcasted_iota(jnp.int32, logits.shape, 1)
    lane_f = lane.astype(F32)
    logits = jnp.where(lane < N_EXPERTS, logits, -jnp.inf)
    m1 = jnp.max(logits, axis=-1, keepdims=True)
    i1 = jnp.min(jnp.where(logits == m1, lane_f, float(LANES)), axis=-1, keepdims=True)
    rest = jnp.where(lane_f == i1, -jnp.inf, logits)
    m2 = jnp.max(rest, axis=-1, keepdims=True)
    i2 = jnp.min(jnp.where(rest == m2, lane_f, float(LANES)), axis=-1, keepdims=True)
    e = jnp.exp(m2 - m1)
    w1 = 1.0 / (1.0 + e)
    w2 = e / (1.0 + e)
    route = jnp.where(lane == 0, i1, 0.0)
    route = jnp.where(lane == 1, i2, route)
    route = jnp.where(lane == 2, w1, route)
    route_ref[...] = jnp.where(lane == 3, w2, route)


def _merge_specs(T, S, tm):
    D = D_MODEL
    tok = lambda i: (i, 0)
    halo_blocks = tm // POOL_HALO
    in_specs = [pl.BlockSpec((tm, D), tok),
                pl.BlockSpec((tm, POOL_WIDTH), tok),
                pl.BlockSpec((POOL_HALO, POOL_WIDTH), lambda i: (jnp.maximum(i * halo_blocks - 1, 0), 0)),
                pl.BlockSpec((tm, FOX_WIDTH), tok),
                pl.BlockSpec((tm, MEM_WIDTH), tok),
                _resident((1, D)), _resident((D, N_BRANCH * D)),
                _resident((len(POOL_WINDOWS), POOL_GROUP, POOL_GROUP)), _resident((1, POOL_WIDTH)),
                _resident((POOL_WIDTH, D)), _resident((FOX_WIDTH, D)), _resident((MEM_WIDTH, D)),
                _resident((D, D)), _resident((1, D))]
    return in_specs, tok


def _merge_ffn(x2d, S, u, fox_o, mem_o, g, wg, pw, ps, wp, wfb, wmb, wo, g2, wgu, wd):
    T, D = x2d.shape
    tm = TM_MERGE
    in_specs, tok = _merge_specs(T, S, tm)
    in_specs += [_resident((D, 2 * D_FF)), _resident((D_FF, D))]
    return pl.pallas_call(
        functools.partial(_merge_ffn_kernel, S // tm),
        grid=(T // tm,),
        in_specs=in_specs,
        out_specs=pl.BlockSpec((tm, D), tok),
        out_shape=jax.ShapeDtypeStruct((T, D), F32),
        scratch_shapes=[pltpu.VMEM((tm, D_FF), BF)],
        compiler_params=_params("arbitrary"),
        name="merge_ffn",
    )(x2d, u, u, fox_o, mem_o, g, wg, pw, ps, wp, wfb, wmb, wo, g2, wgu, wd)


def _merge_router(x2d, S, u, fox_o, mem_o, g, wg, pw, ps, wp, wfb, wmb, wo, g2, wr, br):
    T, D = x2d.shape
    tm = TM_MERGE
    in_specs, tok = _merge_specs(T, S, tm)
    in_specs += [_resident((D, LANES)), _resident((1, LANES))]
    return pl.pallas_call(
        functools.partial(_merge_router_kernel, S // tm),
        grid=(T // tm,),
        in_specs=in_specs,
        out_specs=[pl.BlockSpec((tm, D), tok), pl.BlockSpec((tm, D), tok),
                   pl.BlockSpec((tm, LANES), tok)],
        out_shape=[jax.ShapeDtypeStruct((T, D), F32), jax.ShapeDtypeStruct((T, D), F32),
                   jax.ShapeDtypeStruct((T, LANES), F32)],
        compiler_params=_params("arbitrary"),
        name="merge_router",
    )(x2d, u, u, fox_o, mem_o, g, wg, pw, ps, wp, wfb, wmb, wo, g2, wr, br)


def _gather_kernel(idx_ref, src_ref, out_ref, sem):
    rows = out_ref.shape[0]

    def row_copy(src_row, dst_row):
        return pltpu.make_async_copy(src_ref.at[pl.ds(src_row, 1)], out_ref.at[pl.ds(dst_row, 1)], sem)

    def issue(r, carry):
        row_copy(idx_ref[0, 0, r], r).start()
        return carry

    def drain(r, carry):
        row_copy(0, r).wait()
        return carry

    lax.fori_loop(0, rows, issue, 0, unroll=8)
    lax.fori_loop(0, rows, drain, 0, unroll=8)


def _gather_rows(src, idx):
    n = idx.shape[0]
    D = src.shape[1]
    R = GATHER_ROWS
    return pl.pallas_call(
        _gather_kernel,
        grid=(n // R,),
        in_specs=[pl.BlockSpec((1, 1, R), lambda i: (i, 0, 0), memory_space=pltpu.SMEM),
                  pl.BlockSpec(memory_space=pl.ANY)],
        out_specs=pl.BlockSpec((R, D), lambda i: (i, 0)),
        out_shape=jax.ShapeDtypeStruct((n, D), src.dtype),
        scratch_shapes=[pltpu.SemaphoreType.DMA],
        compiler_params=_params("arbitrary"),
        name="row_gather",
    )(idx.reshape(n // R, 1, R), src)


def _expert_kernel(te_ref, nu_ref, xs_ref, wgu_ref, wd_ref, y_ref, act_ref):
    i = pl.program_id(0)

    @pl.when(i < nu_ref[0])
    def _():
        y_ref[...] = _swiglu(xs_ref[...].astype(BF), wgu_ref, wd_ref, act_ref, D_EXPERT)

    @pl.when(i >= nu_ref[0])
    def _():
        y_ref[...] = jnp.zeros_like(y_ref)


def _experts(xs, tile_expert, n_used, wgu, wd):
    n, D = xs.shape
    tm = TM_EXPERT
    grid_spec = pltpu.PrefetchScalarGridSpec(
        num_scalar_prefetch=2,
        grid=(n // tm,),
        in_specs=[pl.BlockSpec((tm, D), lambda i, te, nu: (i, 0)),
                  pl.BlockSpec((None, D, 2 * D_EXPERT), lambda i, te, nu: (te[i], 0, 0)),
                  pl.BlockSpec((None, D_EXPERT, D), lambda i, te, nu: (te[i], 0, 0))],
        out_specs=pl.BlockSpec((tm, D), lambda i, te, nu: (i, 0)),
        scratch_shapes=[pltpu.VMEM((tm, D_EXPERT), BF)],
    )
    return pl.pallas_call(
        _expert_kernel,
        grid_spec=grid_spec,
        out_shape=jax.ShapeDtypeStruct((n, D), F32),
        compiler_params=_params("arbitrary"),
        name="experts",
    )(tile_expert, n_used, xs, wgu, wd)


def _combine_kernel(x1_ref, route_ref, y0_ref, y1_ref, o_ref):
    r = route_ref[...]
    o_ref[...] = x1_ref[...] + r[:, 2:3] * y0_ref[...] + r[:, 3:4] * y1_ref[...]


def _combine(x1, route, yg):
    T, D = x1.shape
    tm = TM_MERGE
    nt = T // tm
    return pl.pallas_call(
        _combine_kernel,
        grid=(nt,),
        in_specs=[pl.BlockSpec((tm, D), lambda i: (i, 0)),
                  pl.BlockSpec((tm, LANES), lambda i: (i, 0)),
                  pl.BlockSpec((tm, D), lambda i: (i, 0)),
                  pl.BlockSpec((tm, D), lambda i: (i + nt, 0))],
        out_specs=pl.BlockSpec((tm, D), lambda i: (i, 0)),
        out_shape=jax.ShapeDtypeStruct((T, D), F32),
        compiler_params=_params("arbitrary"),
        name="moe_combine",
    )(x1, route, yg, yg)


def _dispatch_plan(route, T):
    tm = TM_EXPERT
    n_slots = TOP_K * T + N_EXPERTS * tm
    n_tiles = n_slots // tm
    e = route[:, :TOP_K].astype(jnp.int32).reshape(-1)
    onehot = (e[:, None] == jnp.arange(N_EXPERTS, dtype=jnp.int32)[None, :]).astype(jnp.int32)
    csum = jnp.cumsum(onehot, axis=0)
    counts = csum[-1]
    rank = jnp.take_along_axis(csum, e[:, None], axis=1)[:, 0] - 1
    padded = ((counts + tm - 1) // tm) * tm
    ends = jnp.cumsum(padded)
    pos = (ends - padded)[e] + rank
    token = jnp.arange(TOP_K * T, dtype=jnp.int32) // TOP_K
    src_token = jnp.zeros((n_slots,), jnp.int32).at[pos].set(token)
    tile_ends = ends // tm
    tile_expert = jnp.minimum(
        jnp.searchsorted(tile_ends, jnp.arange(n_tiles, dtype=jnp.int32), side="right"),
        N_EXPERTS - 1).astype(jnp.int32)
    n_used = tile_ends[-1:].astype(jnp.int32)
    pos_by_k = pos.reshape(T, TOP_K).T.reshape(-1).astype(jnp.int32)
    return src_token, tile_expert, n_used, pos_by_k


def _row(v, width=None):
    v = v.astype(F32).reshape(1, -1)
    if width is not None and v.shape[1] < width:
        v = jnp.pad(v, ((0, 0), (0, width - v.shape[1])))
    return v


def kernel(x, mem, mix_norm_g, w_in, b_forget, fox_q_g, fox_k_g, pool_w, pool_scale, mem_norm_g,
           w_mem_kv, mem_q_g, mem_k_g, w_pool_br, w_fox_br, w_mem_br, w_out, ffn_norm_g, w_ffn_gu,
           w_ffn_down, w_router, b_router, w_exp_gu, w_exp_down):
    B, S, D = x.shape
    T = B * S
    depth = w_in.shape[0]
    assert D == D_MODEL and S % TM_IN == 0 and S % TM_MERGE == 0 and mem.shape[1] == N_MEM
    nk = S // TQ
    head_mean = jnp.kron(jnp.eye(FOX_HEADS, dtype=F32),
                         jnp.full((FOX_HEAD_DIM, FOX_HEAD_DIM), 1.0 / FOX_HEAD_DIM, F32)).astype(BF)

    x2d = x.reshape(T, D)
    for layer in range(depth):
        w = w_in[layer]
        wa = jnp.concatenate([w[:, OFF_POOL:OFF_F], w[:, OFF_MQ:OFF_G]], axis=1).astype(BF)
        wf = jnp.pad(w[:, OFF_F:OFF_MQ], ((0, 0), (0, LANES - FOX_HEADS))).astype(BF)
        wg = w[:, OFF_G:].astype(BF)
        qg = _row(jnp.tile(fox_q_g[layer], FOX_HEADS) * FOX_HEAD_DIM ** -0.5)
        kg = _row(jnp.tile(fox_k_g[layer], FOX_HEADS))
        mqg = _row(mem_q_g[layer] * MEM_HEAD_DIM ** -0.5)

        mkT, mv = _mem_kv(mem, _row(mem_norm_g[layer]), w_mem_kv[layer].astype(BF), _row(mem_k_g[layer]))
        u, q, kT, v, logf, mem_o = _in_proj(
            x2d, B, S, _row(mix_norm_g[layer]), wa, wf, _row(b_forget[layer], LANES), qg, kg,
            head_mean, mqg, mkT, mv)
        c = _cumsum(logf)
        c = c.reshape(B, FOX_HEADS // 2, 2, nk, TQ).transpose(0, 1, 3, 2, 4)
        fox_o = _fox(q.reshape(B, S, FOX_WIDTH), kT, v.reshape(B, S, FOX_WIDTH), c).reshape(T, FOX_WIDTH)

        merge_args = (x2d, S, u, fox_o, mem_o, _row(mix_norm_g[layer]), wg, pool_w[layer].astype(BF),
                      _row(pool_scale[layer]), w_pool_br[layer].astype(BF), w_fox_br[layer].astype(BF),
                      w_mem_br[layer].astype(BF), w_out[layer].astype(BF), _row(ffn_norm_g[layer]))
        if layer % 2 == 0:
            x2d = _merge_ffn(*merge_args, w_ffn_gu[layer // 2].astype(BF), w_ffn_down[layer // 2].astype(BF))
        else:
            m = layer // 2
            wr = jnp.pad(w_router[m], ((0, 0), (0, LANES - N_EXPERTS)))
            x1, h2, route = _merge_router(*merge_args, wr, _row(b_router[m], LANES))
            src_token, tile_expert, n_used, pos_by_k = _dispatch_plan(route, T)
            xs = _gather_rows(h2, src_token)
            y = _experts(xs, tile_expert, n_used, w_exp_gu[m].astype(BF), w_exp_down[m].astype(BF))
            x2d = _combine(x1, route, _gather_rows(y, pos_by_k))
    return x2d.reshape(B, S, D)
```

```python
import functools

import jax
import jax.numpy as jnp
from jax import lax
from jax.experimental import pallas as pl
from jax.experimental.pallas import tpu as pltpu

D_MODEL = 1024
N_MEM = 256
POOL_WINDOWS = (2, 4, 8, 16)
POOL_WIDTH = 512
POOL_GROUP = 128
POOL_HALO = 16
FOX_HEADS = 8
FOX_HEAD_DIM = 64
FOX_WIDTH = 512
MEM_HEADS = 4
MEM_HEAD_DIM = 128
MEM_WIDTH = 512
N_BRANCH = 3
D_FF = 2816
N_EXPERTS = 8
TOP_K = 2
D_EXPERT = 3584
EPS = 1e-6

OFF_POOL = 0
OFF_Q = OFF_POOL + POOL_WIDTH
OFF_K = OFF_Q + FOX_WIDTH
OFF_V = OFF_K + FOX_WIDTH
OFF_F = OFF_V + FOX_WIDTH
OFF_MQ = OFF_F + FOX_HEADS
OFF_G = OFF_MQ + MEM_WIDTH

LANES = 128
LOG2E = 1.4426950408889634
BIAS_TERMS = 3
BIAS_ROWS = 16
SAFE_EXP2_SPAN = 100.0
NORM_SLACK = 1.02
BF = jnp.bfloat16
F32 = jnp.float32

TM_IN = 512
TQ = 512
TM_MERGE = 256
TM_EXPERT = 256
GATHER_ROWS = 256
FF_CHUNK = 256
VMEM_LIMIT = 58 * 1024 * 1024


def _resident(shape):
    nd = len(shape)
    return pl.BlockSpec(shape, lambda *_: (0,) * nd, pipeline_mode=pl.Buffered(1))


def _params(*sem):
    return pltpu.CompilerParams(dimension_semantics=sem, vmem_limit_bytes=VMEM_LIMIT)


def _dot(a, b):
    return jnp.dot(a, b, preferred_element_type=F32)


def _rms(x, g):
    ms = jnp.mean(x * x, axis=-1, keepdims=True)
    return x * lax.rsqrt(ms + EPS) * g


def _sigmoid(z):
    return 1.0 / (1.0 + jnp.exp(-z))


def _mem_kv_kernel(mem_ref, g_ref, w_ref, kg_ref, kT_ref, v_ref):
    mn = _rms(mem_ref[...], g_ref[...]).astype(BF)
    kv = _dot(mn, w_ref[...])
    for h in range(MEM_HEADS):
        kh = kv[:, h * MEM_HEAD_DIM:(h + 1) * MEM_HEAD_DIM]
        kT_ref[h] = _rms(kh, kg_ref[...]).T.astype(BF)
    v_ref[...] = kv[:, MEM_WIDTH:].astype(BF)


def _mem_kv(mem, g, w_kv, k_g):
    B, M, D = mem.shape
    return pl.pallas_call(
        _mem_kv_kernel,
        grid=(B,),
        in_specs=[pl.BlockSpec((None, M, D), lambda b: (b, 0, 0)),
                  _resident((1, D)), _resident((D, 2 * MEM_WIDTH)), _resident((1, MEM_HEAD_DIM))],
        out_specs=[pl.BlockSpec((None, MEM_HEADS, MEM_HEAD_DIM, M), lambda b: (b, 0, 0, 0)),
                   pl.BlockSpec((None, M, MEM_WIDTH), lambda b: (b, 0, 0))],
        out_shape=[jax.ShapeDtypeStruct((B, MEM_HEADS, MEM_HEAD_DIM, M), BF),
                   jax.ShapeDtypeStruct((B, M, MEM_WIDTH), BF)],
        compiler_params=_params("arbitrary"),
        name="mem_kv",
    )(mem, g, w_kv, k_g)


def _in_proj_kernel(x_ref, g_ref, wa_ref, wf_ref, bf_ref, qg_ref, kg_ref, bd_ref, mqg_ref,
                    mkT_ref, mv_ref, u_ref, q_ref, kT_ref, v_ref, lf_ref, mo_ref):
    tm = x_ref.shape[0]
    hb = _rms(x_ref[...], g_ref[...]).astype(BF)

    u_ref[...] = _dot(hb, wa_ref[:, OFF_POOL:OFF_Q]).astype(BF)

    lane = lax.broadcasted_iota(jnp.int32, (tm, LANES), 1)
    live = lane < FOX_HEAD_DIM

    def split_heads(z, n_ones, ref):
        pad = jnp.where(lane < FOX_HEAD_DIM + n_ones, 1.0, 0.0)
        for p in range(FOX_HEADS // 2):
            pair = z[:, p * LANES:(p + 1) * LANES]
            ref[2 * p] = jnp.where(live, pair, pad).astype(BF)
            ref[2 * p + 1] = jnp.where(live, pltpu.roll(pair, FOX_HEAD_DIM, axis=1), pad).astype(BF)

    bd = bd_ref[...]
    zq = _dot(hb, wa_ref[:, OFF_Q:OFF_K])
    qms = _dot((zq * zq).astype(BF), bd)
    split_heads(zq * lax.rsqrt(qms + EPS) * qg_ref[...], BIAS_TERMS, q_ref)
    zk = _dot(hb, wa_ref[:, OFF_K:OFF_V])
    kms = _dot((zk * zk).astype(BF), bd)
    kn = zk * lax.rsqrt(kms + EPS) * kg_ref[...]
    kT_ref[:, 0:FOX_HEAD_DIM, :] = kn.T.reshape(FOX_HEADS, FOX_HEAD_DIM, tm).astype(BF)
    kT_ref[:, FOX_HEAD_DIM:, :] = jnp.zeros((FOX_HEADS, LANES - FOX_HEAD_DIM, tm), BF)
    split_heads(_dot(hb, wa_ref[:, OFF_V:OFF_F]), 1, v_ref)

    zf = _dot(hb, wf_ref[...]) + bf_ref[...]
    logf = jnp.minimum(zf, 0.0) - jnp.log(1.0 + jnp.exp(-jnp.abs(zf)))
    lf_ref[...] = logf.T[0:FOX_HEADS, :]

    zm = _dot(hb, wa_ref[:, OFF_F:OFF_F + MEM_WIDTH])
    for h in range(MEM_HEADS):
        sl = slice(h * MEM_HEAD_DIM, (h + 1) * MEM_HEAD_DIM)
        qn = _rms(zm[:, sl], mqg_ref[...]).astype(BF)
        s = _dot(qn, mkT_ref[h])
        p = jnp.exp(s - jnp.max(s, axis=-1, keepdims=True))
        l = jnp.sum(p, axis=-1, keepdims=True)
        mo_ref[:, sl] = (_dot(p.astype(BF), mv_ref[:, sl]) / l).astype(BF)


def _in_proj(x2d, B, S, g, wa, wf, bfg, qg, kg, bd, mqg, mkT, mv):
    T, D = x2d.shape
    tm = TM_IN
    nk = S // tm
    wa_cols = wa.shape[1]
    M = mkT.shape[-1]
    tok = lambda i: (i, 0)
    return pl.pallas_call(
        _in_proj_kernel,
        grid=(T // tm,),
        in_specs=[pl.BlockSpec((tm, D), tok),
                  _resident((1, D)), _resident((D, wa_cols)), _resident((D, LANES)),
                  _resident((1, LANES)), _resident((1, FOX_WIDTH)), _resident((1, FOX_WIDTH)),
                  _resident((FOX_WIDTH, FOX_WIDTH)), _resident((1, MEM_HEAD_DIM)),
                  pl.BlockSpec((None, MEM_HEADS, MEM_HEAD_DIM, M), lambda i: (i // nk, 0, 0, 0)),
                  pl.BlockSpec((None, M, MEM_WIDTH), lambda i: (i // nk, 0, 0))],
        out_specs=[pl.BlockSpec((tm, POOL_WIDTH), tok),
                   pl.BlockSpec((None, FOX_HEADS, tm, LANES), lambda i: (i // nk, 0, i % nk, 0)),
                   pl.BlockSpec((None, FOX_HEADS, None, LANES, tm), lambda i: (i // nk, 0, i % nk, 0, 0)),
                   pl.BlockSpec((None, FOX_HEADS, tm, LANES), lambda i: (i // nk, 0, i % nk, 0)),
                   pl.BlockSpec((None, FOX_HEADS, tm), lambda i: (i // nk, 0, i % nk)),
                   pl.BlockSpec((tm, MEM_WIDTH), tok)],
        out_shape=[jax.ShapeDtypeStruct((T, POOL_WIDTH), BF),
                   jax.ShapeDtypeStruct((B, FOX_HEADS, S, LANES), BF),
                   jax.ShapeDtypeStruct((B, FOX_HEADS, nk, LANES, tm), BF),
                   jax.ShapeDtypeStruct((B, FOX_HEADS, S, LANES), BF),
                   jax.ShapeDtypeStruct((B, FOX_HEADS, S), F32),
                   jax.ShapeDtypeStruct((T, MEM_WIDTH), BF)],
        compiler_params=_params("arbitrary"),
        name="in_proj",
    )(x2d, g, wa, wf, bfg, qg, kg, bd, mqg, mkT, mv)


def _split_terms(x, n):
    terms = []
    for _ in range(n):
        t = x.astype(BF).astype(F32)
        terms.append(t)
        x = x - t
    return terms


def _cumsum_kernel(lf_ref, k_any_ref, krow_ref):
    del k_any_ref
    x = lf_ref[...]
    H, S = x.shape
    tk = krow_ref.shape[-1]
    lane = lax.broadcasted_iota(jnp.int32, x.shape, 1)
    k = 1
    while k < S:
        x = x + jnp.where(lane >= k, pltpu.roll(x, k, axis=1), 0.0)
        k *= 2
    terms = _split_terms(x * (-LOG2E), BIAS_TERMS)
    sub = lax.broadcasted_iota(jnp.int32, (BIAS_ROWS, tk), 0)
    ones_rows = jnp.where((sub >= BIAS_TERMS) & (sub < 2 * BIAS_TERMS), 1.0, 0.0)
    for j in range(S // tk):
        sl = slice(j * tk, (j + 1) * tk)
        for h in range(H):
            blk = ones_rows
            for r, t in enumerate(terms):
                blk = jnp.where(sub == r, t[h:h + 1, sl], blk)
            krow_ref[h, j] = blk.astype(BF)


def _cumsum(logf, kT):
    B, H, S = logf.shape
    nk, tk = kT.shape[2], kT.shape[4]
    assert FOX_HEAD_DIM % BIAS_ROWS == 0 and 2 * BIAS_TERMS <= BIAS_ROWS
    return pl.pallas_call(
        _cumsum_kernel, grid=(B,),
        in_specs=[pl.BlockSpec((None, H, S), lambda b: (b, 0, 0)),
                  pl.BlockSpec(memory_space=pl.ANY)],
        out_specs=pl.BlockSpec((None, H, nk, BIAS_ROWS, tk),
                               lambda b: (b, 0, 0, FOX_HEAD_DIM // BIAS_ROWS, 0)),
        out_shape=jax.ShapeDtypeStruct(kT.shape, kT.dtype),
        input_output_aliases={1: 0},
        compiler_params=_params("arbitrary"), name="forget_cumsum",
    )(logf, kT)


def _fox_kernel(exact_max, ub_ref, q_ref, kT_ref, v_ref, o_ref):
    tq = q_ref.shape[1]
    tk = kT_ref.shape[-1]
    i = pl.program_id(2)
    row = lax.broadcasted_iota(jnp.int32, (tq, tk), 0)
    col = lax.broadcasted_iota(jnp.int32, (tq, tk), 1)
    causal = col <= row
    lane = lax.broadcasted_iota(jnp.int32, (tq, LANES), 1)
    bias_lanes = (lane >= FOX_HEAD_DIM) & (lane < FOX_HEAD_DIM + BIAS_TERMS)

    def scores(q, h, j, diagonal):
        z = _dot(q, kT_ref[h, j])
        return jnp.where(causal, z, -jnp.inf) if diagonal else z

    def max_step(j, carry, diagonal):
        out = []
        for h in range(2):
            q = q_ref[h] if exact_max else jnp.where(bias_lanes, 1.0, 0.0).astype(BF)
            z = scores(q, h, j, diagonal)
            m = carry[h]
            for c in range(tk // LANES):
                m = jnp.maximum(m, z[:, c * LANES:(c + 1) * LANES])
            out.append(m)
        return tuple(out)

    m_lane = tuple(jnp.full((tq, LANES), -jnp.inf, F32) for _ in range(2))
    if exact_max:
        m_lane = lax.fori_loop(0, i, lambda j, c: max_step(j, c, False), m_lane)
    m_lane = max_step(i, m_lane, True)
    q_stab = []
    for h in range(2):
        m_row = jnp.max(m_lane[h], axis=-1, keepdims=True)
        if not exact_max:
            m_row = m_row + ub_ref[0]
        q = q_ref[h]
        for r, t in enumerate(_split_terms(-m_row, BIAS_TERMS)):
            q = jnp.where(lane == FOX_HEAD_DIM + BIAS_TERMS + r, t.astype(BF), q)
        q_stab.append(q)

    def pv_step(j, carry, diagonal):
        rows = pl.ds(pl.multiple_of(j * tk, tk), tk)
        out = []
        for h in range(2):
            p = jnp.exp2(scores(q_stab[h], h, j, diagonal))
            out.append(carry[h] + _dot(p.astype(BF), v_ref[h, rows, :]))
        return tuple(out)

    a_init = tuple(jnp.zeros((tq, LANES), F32) for _ in range(2))
    a0, a1 = pv_step(i, lax.fori_loop(0, i, lambda j, c: pv_step(j, c, False), a_init), True)
    o0 = a0 / a0[:, FOX_HEAD_DIM:FOX_HEAD_DIM + 1]
    o1 = a1 / a1[:, FOX_HEAD_DIM:FOX_HEAD_DIM + 1]
    o_ref[...] = jnp.where(lane < FOX_HEAD_DIM, o0, pltpu.roll(o1, FOX_HEAD_DIM, axis=1)).astype(BF)


def _fox_call(exact_max, ub, q, kT, v):
    B, H, S, _ = q.shape
    nk = S // TQ
    grid_spec = pltpu.PrefetchScalarGridSpec(
        num_scalar_prefetch=1,
        grid=(B, H // 2, nk),
        in_specs=[pl.BlockSpec((None, 2, TQ, LANES), lambda b, h, i, ub: (b, h, i, 0)),
                  pl.BlockSpec((None, 2, nk, LANES, TQ), lambda b, h, i, ub: (b, h, 0, 0, 0)),
                  pl.BlockSpec((None, 2, S, LANES), lambda b, h, i, ub: (b, h, 0, 0))],
        out_specs=pl.BlockSpec((None, TQ, LANES), lambda b, h, i, ub: (b, i, h)),
    )
    return pl.pallas_call(
        functools.partial(_fox_kernel, exact_max),
        grid_spec=grid_spec,
        out_shape=jax.ShapeDtypeStruct((B, S, H * FOX_HEAD_DIM), BF),
        compiler_params=_params("arbitrary", "arbitrary", "arbitrary"),
        name="fox_attention_exact_max" if exact_max else "fox_attention",
    )(ub, q, kT, v)


def _fox(score_bound, q, kT, v):
    ub = score_bound.reshape(1).astype(F32)
    return lax.cond(2.0 * score_bound <= SAFE_EXP2_SPAN,
                    functools.partial(_fox_call, False), functools.partial(_fox_call, True),
                    ub, q, kT, v)


def _merge_body(n_seq_tiles, x_ref, u_ref, up_ref, fox_ref, mo_ref, g_ref, wg_ref, pw_ref, ps_ref,
                wp_ref, wfb_ref, wmb_ref, wo_ref):
    tm = x_ref.shape[0]
    i = pl.program_id(0)
    x = x_ref[...]
    hb = _rms(x, g_ref[...]).astype(BF)

    seq_tile = i % n_seq_tiles
    halo = jnp.where(seq_tile == 0, 0.0, up_ref[...].astype(F32))
    u = u_ref[...].astype(F32)
    ext = jnp.concatenate([halo, u], axis=0)
    pos = seq_tile * tm + lax.broadcasted_iota(jnp.int32, (tm, 1), 0)
    outs = []
    for g, w in enumerate(POOL_WINDOWS):
        sl = slice(g * POOL_GROUP, (g + 1) * POOL_GROUP)
        s = ext[:, sl]
        k = 1
        while k < w:
            s = s + pltpu.roll(s, k, axis=0)
            k *= 2
        cnt = jnp.minimum(pos + 1, w).astype(F32)
        d = s[POOL_HALO:, :] / cnt - u[:, sl]
        outs.append(_dot(d.astype(BF), pw_ref[g]))
    pool_o = (jnp.concatenate(outs, axis=1) * ps_ref[...]).astype(BF)

    merged = _sigmoid(_dot(hb, wg_ref[:, 0:D_MODEL])) * _dot(pool_o, wp_ref[...])
    merged += _sigmoid(_dot(hb, wg_ref[:, D_MODEL:2 * D_MODEL])) * _dot(fox_ref[...], wfb_ref[...])
    merged += _sigmoid(_dot(hb, wg_ref[:, 2 * D_MODEL:])) * _dot(mo_ref[...], wmb_ref[...])
    return x + _dot(merged.astype(BF), wo_ref[...])


def _swiglu(hb, wgu_ref, wd_ref, act_ref, d_hidden):
    for c in range(d_hidden // FF_CHUNK):
        lo = c * FF_CHUNK
        g = _dot(hb, wgu_ref[:, lo:lo + FF_CHUNK])
        up = _dot(hb, wgu_ref[:, d_hidden + lo:d_hidden + lo + FF_CHUNK])
        act_ref[:, lo:lo + FF_CHUNK] = (g * _sigmoid(g) * up).astype(BF)
    return _dot(act_ref[...], wd_ref[...])


def _merge_ffn_kernel(n_seq_tiles, x_ref, u_ref, up_ref, fox_ref, mo_ref, g_ref, wg_ref, pw_ref,
                      ps_ref, wp_ref, wfb_ref, wmb_ref, wo_ref, g2_ref, wgu_ref, wd_ref,
                      o_ref, act_ref):
    x1 = _merge_body(n_seq_tiles, x_ref, u_ref, up_ref, fox_ref, mo_ref, g_ref, wg_ref, pw_ref,
                     ps_ref, wp_ref, wfb_ref, wmb_ref, wo_ref)
    h2 = _rms(x1, g2_ref[...]).astype(BF)
    o_ref[...] = x1 + _swiglu(h2, wgu_ref, wd_ref, act_ref, D_FF)


def _split_bf16(a):
    hi = a.astype(BF)
    return hi, (a - hi.astype(F32)).astype(BF)


def _merge_router_kernel(n_seq_tiles, x_ref, u_ref, up_ref, fox_ref, mo_ref, g_ref, wg_ref, pw_ref,
                         ps_ref, wp_ref, wfb_ref, wmb_ref, wo_ref, g2_ref, wr_ref, br_ref,
                         x1_ref, h2_ref, route_ref):
    x1 = _merge_body(n_seq_tiles, x_ref, u_ref, up_ref, fox_ref, mo_ref, g_ref, wg_ref, pw_ref,
                     ps_ref, wp_ref, wfb_ref, wmb_ref, wo_ref)
    x1_ref[...] = x1
    h2 = _rms(x1, g2_ref[...])
    h2_ref[...] = h2

    h_hi, h_lo = _split_bf16(h2)
    w_hi, w_lo = _split_bf16(wr_ref[...])
    logits = _dot(h_hi, w_hi) + (_dot(h_hi, w_lo) + _dot(h_lo, w_hi)) + br_ref[...]
    lane = lax.broadcasted_iota(jnp.int32, logits.shape, 1)
    lane_f = lane.astype(F32)
    logits = jnp.where(lane < N_EXPERTS, logits, -jnp.inf)
    m1 = jnp.max(logits, axis=-1, keepdims=True)
    i1 = jnp.min(jnp.where(logits == m1, lane_f, float(LANES)), axis=-1, keepdims=True)
    rest = jnp.where(lane_f == i1, -jnp.inf, logits)
    m2 = jnp.max(rest, axis=-1, keepdims=True)
    i2 = jnp.min(jnp.where(rest == m2, lane_f, float(LANES)), axis=-1, keepdims=True)
    e = jnp.exp(m2 - m1)
    w1 = 1.0 / (1.0 + e)
    w2 = e / (1.0 + e)
    route = jnp.where(lane == 0, i1, 0.0)
    route = jnp.where(lane == 1, i2, route)
    route = jnp.where(lane == 2, w1, route)
    route_ref[...] = jnp.where(lane == 3, w2, route)


def _merge_specs(T, S, tm):
    D = D_MODEL
    tok = lambda i: (i, 0)
    halo_blocks = tm // POOL_HALO
    in_specs = [pl.BlockSpec((tm, D), tok),
                pl.BlockSpec((tm, POOL_WIDTH), tok),
                pl.BlockSpec((POOL_HALO, POOL_WIDTH), lambda i: (jnp.maximum(i * halo_blocks - 1, 0), 0)),
                pl.BlockSpec((tm, FOX_WIDTH), tok),
                pl.BlockSpec((tm, MEM_WIDTH), tok),
                _resident((1, D)), _resident((D, N_BRANCH * D)),
                _resident((len(POOL_WINDOWS), POOL_GROUP, POOL_GROUP)), _resident((1, POOL_WIDTH)),
                _resident((POOL_WIDTH, D)), _resident((FOX_WIDTH, D)), _resident((MEM_WIDTH, D)),
                _resident((D, D)), _resident((1, D))]
    return in_specs, tok


def _merge_ffn(x2d, S, u, fox_o, mem_o, g, wg, pw, ps, wp, wfb, wmb, wo, g2, wgu, wd):
    T, D = x2d.shape
    tm = TM_MERGE
    in_specs, tok = _merge_specs(T, S, tm)
    in_specs += [_resident((D, 2 * D_FF)), _resident((D_FF, D))]
    return pl.pallas_call(
        functools.partial(_merge_ffn_kernel, S // tm),
        grid=(T // tm,),
        in_specs=in_specs,
        out_specs=pl.BlockSpec((tm, D), tok),
        out_shape=jax.ShapeDtypeStruct((T, D), F32),
        scratch_shapes=[pltpu.VMEM((tm, D_FF), BF)],
        compiler_params=_params("arbitrary"),
        name="merge_ffn",
    )(x2d, u, u, fox_o, mem_o, g, wg, pw, ps, wp, wfb, wmb, wo, g2, wgu, wd)


def _merge_router(x2d, S, u, fox_o, mem_o, g, wg, pw, ps, wp, wfb, wmb, wo, g2, wr, br):
    T, D = x2d.shape
    tm = TM_MERGE
    in_specs, tok = _merge_specs(T, S, tm)
    in_specs += [_resident((D, LANES)), _resident((1, LANES))]
    return pl.pallas_call(
        functools.partial(_merge_router_kernel, S // tm),
        grid=(T // tm,),
        in_specs=in_specs,
        out_specs=[pl.BlockSpec((tm, D), tok), pl.BlockSpec((tm, D), tok),
                   pl.BlockSpec((tm, LANES), tok)],
        out_shape=[jax.ShapeDtypeStruct((T, D), F32), jax.ShapeDtypeStruct((T, D), F32),
                   jax.ShapeDtypeStruct((T, LANES), F32)],
        compiler_params=_params("arbitrary"),
        name="merge_router",
    )(x2d, u, u, fox_o, mem_o, g, wg, pw, ps, wp, wfb, wmb, wo, g2, wr, br)


def _gather_kernel(idx_ref, src_ref, out_ref, sem):
    rows = out_ref.shape[0]

    def row_copy(src_row, dst_row):
        return pltpu.make_async_copy(src_ref.at[pl.ds(src_row, 1)], out_ref.at[pl.ds(dst_row, 1)], sem)

    def issue(r, carry):
        row_copy(idx_ref[0, 0, r], r).start()
        return carry

    def drain(r, carry):
        row_copy(0, r).wait()
        return carry

    lax.fori_loop(0, rows, issue, 0, unroll=8)
    lax.fori_loop(0, rows, drain, 0, unroll=8)


def _gather_rows(src, idx):
    n = idx.shape[0]
    D = src.shape[1]
    R = GATHER_ROWS
    return pl.pallas_call(
        _gather_kernel,
        grid=(n // R,),
        in_specs=[pl.BlockSpec((1, 1, R), lambda i: (i, 0, 0), memory_space=pltpu.SMEM),
                  pl.BlockSpec(memory_space=pl.ANY)],
        out_specs=pl.BlockSpec((R, D), lambda i: (i, 0)),
        out_shape=jax.ShapeDtypeStruct((n, D), src.dtype),
        scratch_shapes=[pltpu.SemaphoreType.DMA],
        compiler_params=_params("arbitrary"),
        name="row_gather",
    )(idx.reshape(n // R, 1, R), src)


def _expert_kernel(te_ref, nu_ref, xs_ref, wgu_ref, wd_ref, y_ref, act_ref):
    i = pl.program_id(0)

    @pl.when(i < nu_ref[0])
    def _():
        y_ref[...] = _swiglu(xs_ref[...].astype(BF), wgu_ref, wd_ref, act_ref, D_EXPERT)

    @pl.when(i >= nu_ref[0])
    def _():
        y_ref[...] = jnp.zeros_like(y_ref)


def _experts(xs, tile_expert, n_used, wgu, wd):
    n, D = xs.shape
    tm = TM_EXPERT
    grid_spec = pltpu.PrefetchScalarGridSpec(
        num_scalar_prefetch=2,
        grid=(n // tm,),
        in_specs=[pl.BlockSpec((tm, D), lambda i, te, nu: (i, 0)),
                  pl.BlockSpec((None, D, 2 * D_EXPERT), lambda i, te, nu: (te[i], 0, 0)),
                  pl.BlockSpec((None, D_EXPERT, D), lambda i, te, nu: (te[i], 0, 0))],
        out_specs=pl.BlockSpec((tm, D), lambda i, te, nu: (i, 0)),
        scratch_shapes=[pltpu.VMEM((tm, D_EXPERT), BF)],
    )
    return pl.pallas_call(
        _expert_kernel,
        grid_spec=grid_spec,
        out_shape=jax.ShapeDtypeStruct((n, D), F32),
        compiler_params=_params("arbitrary"),
        name="experts",
    )(tile_expert, n_used, xs, wgu, wd)


def _combine_kernel(x1_ref, route_ref, y0_ref, y1_ref, o_ref):
    r = route_ref[...]
    o_ref[...] = x1_ref[...] + r[:, 2:3] * y0_ref[...] + r[:, 3:4] * y1_ref[...]


def _combine(x1, route, yg):
    T, D = x1.shape
    tm = TM_MERGE
    nt = T // tm
    return pl.pallas_call(
        _combine_kernel,
        grid=(nt,),
        in_specs=[pl.BlockSpec((tm, D), lambda i: (i, 0)),
                  pl.BlockSpec((tm, LANES), lambda i: (i, 0)),
                  pl.BlockSpec((tm, D), lambda i: (i, 0)),
                  pl.BlockSpec((tm, D), lambda i: (i + nt, 0))],
        out_specs=pl.BlockSpec((tm, D), lambda i: (i, 0)),
        out_shape=jax.ShapeDtypeStruct((T, D), F32),
        compiler_params=_params("arbitrary"),
        name="moe_combine",
    )(x1, route, yg, yg)


def _dispatch_plan(route, T):
    tm = TM_EXPERT
    n_slots = TOP_K * T + N_EXPERTS * tm
    n_tiles = n_slots // tm
    e = route[:, :TOP_K].astype(jnp.int32).reshape(-1)
    onehot = (e[:, None] == jnp.arange(N_EXPERTS, dtype=jnp.int32)[None, :]).astype(jnp.int32)
    csum = jnp.cumsum(onehot, axis=0)
    counts = csum[-1]
    rank = jnp.take_along_axis(csum, e[:, None], axis=1)[:, 0] - 1
    padded = ((counts + tm - 1) // tm) * tm
    ends = jnp.cumsum(padded)
    pos = (ends - padded)[e] + rank
    token = jnp.arange(TOP_K * T, dtype=jnp.int32) // TOP_K
    src_token = jnp.zeros((n_slots,), jnp.int32).at[pos].set(token)
    tile_ends = ends // tm
    tile_ids = jnp.arange(n_tiles, dtype=jnp.int32)
    tile_expert = jnp.minimum(jnp.sum((tile_ids[:, None] >= tile_ends[None, :]).astype(jnp.int32), axis=1),
                              N_EXPERTS - 1)
    n_used = tile_ends[-1:].astype(jnp.int32)
    pos_by_k = pos.reshape(T, TOP_K).T.reshape(-1).astype(jnp.int32)
    return src_token, tile_expert, n_used, pos_by_k


def _row(v, width=None):
    v = v.astype(F32).reshape(1, -1)
    if width is not None and v.shape[1] < width:
        v = jnp.pad(v, ((0, 0), (0, width - v.shape[1])))
    return v


def kernel(x, mem, mix_norm_g, w_in, b_forget, fox_q_g, fox_k_g, pool_w, pool_scale, mem_norm_g,
           w_mem_kv, mem_q_g, mem_k_g, w_pool_br, w_fox_br, w_mem_br, w_out, ffn_norm_g, w_ffn_gu,
           w_ffn_down, w_router, b_router, w_exp_gu, w_exp_down):
    B, S, D = x.shape
    T = B * S
    depth = w_in.shape[0]
    assert D == D_MODEL and S % TM_IN == 0 and S % TM_MERGE == 0 and mem.shape[1] == N_MEM
    nk = S // TQ
    head_mean = jnp.kron(jnp.eye(FOX_HEADS, dtype=F32),
                         jnp.full((FOX_HEAD_DIM, FOX_HEAD_DIM), 1.0 / FOX_HEAD_DIM, F32)).astype(BF)

    x2d = x.reshape(T, D)
    for layer in range(depth):
        w = w_in[layer]
        wa = jnp.concatenate([w[:, OFF_POOL:OFF_F], w[:, OFF_MQ:OFF_G]], axis=1).astype(BF)
        wf = jnp.pad(w[:, OFF_F:OFF_MQ], ((0, 0), (0, LANES - FOX_HEADS))).astype(BF)
        wg = w[:, OFF_G:].astype(BF)
        qg = _row(jnp.tile(fox_q_g[layer], FOX_HEADS) * (FOX_HEAD_DIM ** -0.5 * LOG2E))
        kg = _row(jnp.tile(fox_k_g[layer], FOX_HEADS))
        mqg = _row(mem_q_g[layer] * MEM_HEAD_DIM ** -0.5)

        mkT, mv = _mem_kv(mem, _row(mem_norm_g[layer]), w_mem_kv[layer].astype(BF), _row(mem_k_g[layer]))
        u, q, kT, v, logf, mem_o = _in_proj(
            x2d, B, S, _row(mix_norm_g[layer]), wa, wf, _row(b_forget[layer], LANES), qg, kg,
            head_mean, mqg, mkT, mv)
        score_bound = (NORM_SLACK * FOX_HEAD_DIM * jnp.max(jnp.abs(qg)) * jnp.max(jnp.abs(kg)))
        fox_o = _fox(score_bound, q, _cumsum(logf, kT), v).reshape(T, FOX_WIDTH)

        merge_args = (x2d, S, u, fox_o, mem_o, _row(mix_norm_g[layer]), wg, pool_w[layer].astype(BF),
                      _row(pool_scale[layer]), w_pool_br[layer].astype(BF), w_fox_br[layer].astype(BF),
                      w_mem_br[layer].astype(BF), w_out[layer].astype(BF), _row(ffn_norm_g[layer]))
        if layer % 2 == 0:
            x2d = _merge_ffn(*merge_args, w_ffn_gu[layer // 2].astype(BF), w_ffn_down[layer // 2].astype(BF))
        else:
            m = layer // 2
            wr = jnp.pad(w_router[m], ((0, 0), (0, LANES - N_EXPERTS)))
            x1, h2, route = _merge_router(*merge_args, wr, _row(b_router[m], LANES))
            src_token, tile_expert, n_used, pos_by_k = _dispatch_plan(route, T)
            xs = _gather_rows(h2, src_token)
            y = _experts(xs, tile_expert, n_used, w_exp_gu[m].astype(BF), w_exp_down[m].astype(BF))
            x2d = _combine(x1, route, _gather_rows(y, pos_by_k))
    return x2d.reshape(B, S, D)
```

```python
import functools

import jax
import jax.numpy as jnp
from jax import lax
from jax.experimental import pallas as pl
from jax.experimental.pallas import tpu as pltpu

D_MODEL = 1024
N_MEM = 256
POOL_WINDOWS = (2, 4, 8, 16)
POOL_WIDTH = 512
POOL_GROUP = 128
POOL_HALO = 16
FOX_HEADS = 8
FOX_HEAD_DIM = 64
FOX_WIDTH = 512
MEM_HEADS = 4
MEM_HEAD_DIM = 128
MEM_WIDTH = 512
N_BRANCH = 3
D_FF = 2816
N_EXPERTS = 8
TOP_K = 2
D_EXPERT = 3584
EPS = 1e-6

OFF_POOL = 0
OFF_Q = OFF_POOL + POOL_WIDTH
OFF_K = OFF_Q + FOX_WIDTH
OFF_V = OFF_K + FOX_WIDTH
OFF_F = OFF_V + FOX_WIDTH
OFF_MQ = OFF_F + FOX_HEADS
OFF_G = OFF_MQ + MEM_WIDTH

LANES = 128
LOG2E = 1.4426950408889634
BIAS_TERMS = 3
BIAS_ROWS = 16
SAFE_EXP2_SPAN = 100.0
NORM_SLACK = 1.02
BF = jnp.bfloat16
F32 = jnp.float32

TM_IN = 512
TQ = 512
TM_MERGE = 256
TM_EXPERT = 256
DISPATCH_ROWS = 512
FF_CHUNK = 256
VMEM_LIMIT = 58 * 1024 * 1024


def _resident(shape):
    nd = len(shape)
    return pl.BlockSpec(shape, lambda *_: (0,) * nd, pipeline_mode=pl.Buffered(1))


def _params(*sem):
    return pltpu.CompilerParams(dimension_semantics=sem, vmem_limit_bytes=VMEM_LIMIT)


def _dot(a, b):
    return jnp.dot(a, b, preferred_element_type=F32)


def _rms(x, g):
    ms = jnp.mean(x * x, axis=-1, keepdims=True)
    return x * lax.rsqrt(ms + EPS) * g


def _sigmoid(z):
    return 1.0 / (1.0 + jnp.exp(-z))


def _mem_kv_kernel(mem_ref, g_ref, w_ref, kg_ref, kT_ref, v_ref):
    mn = _rms(mem_ref[...], g_ref[...]).astype(BF)
    kv = _dot(mn, w_ref[...])
    for h in range(MEM_HEADS):
        kh = kv[:, h * MEM_HEAD_DIM:(h + 1) * MEM_HEAD_DIM]
        kT_ref[h] = _rms(kh, kg_ref[...]).T.astype(BF)
    v_ref[...] = kv[:, MEM_WIDTH:].astype(BF)


def _mem_kv(mem, g, w_kv, k_g):
    B, M, D = mem.shape
    return pl.pallas_call(
        _mem_kv_kernel,
        grid=(B,),
        in_specs=[pl.BlockSpec((None, M, D), lambda b: (b, 0, 0)),
                  _resident((1, D)), _resident((D, 2 * MEM_WIDTH)), _resident((1, MEM_HEAD_DIM))],
        out_specs=[pl.BlockSpec((None, MEM_HEADS, MEM_HEAD_DIM, M), lambda b: (b, 0, 0, 0)),
                   pl.BlockSpec((None, M, MEM_WIDTH), lambda b: (b, 0, 0))],
        out_shape=[jax.ShapeDtypeStruct((B, MEM_HEADS, MEM_HEAD_DIM, M), BF),
                   jax.ShapeDtypeStruct((B, M, MEM_WIDTH), BF)],
        compiler_params=_params("arbitrary"),
        name="mem_kv",
    )(mem, g, w_kv, k_g)


def _in_proj_kernel(x_ref, g_ref, wa_ref, wf_ref, bf_ref, qg_ref, kg_ref, bd_ref, mqg_ref,
                    mkT_ref, mv_ref, u_ref, q_ref, kT_ref, v_ref, lf_ref, mo_ref):
    tm = x_ref.shape[0]
    hb = _rms(x_ref[...], g_ref[...]).astype(BF)

    u_ref[...] = _dot(hb, wa_ref[:, OFF_POOL:OFF_Q]).astype(BF)

    lane = lax.broadcasted_iota(jnp.int32, (tm, LANES), 1)
    live = lane < FOX_HEAD_DIM

    def split_heads(z, n_ones, ref):
        pad = jnp.where(lane < FOX_HEAD_DIM + n_ones, 1.0, 0.0)
        for p in range(FOX_HEADS // 2):
            pair = z[:, p * LANES:(p + 1) * LANES]
            ref[2 * p] = jnp.where(live, pair, pad).astype(BF)
            ref[2 * p + 1] = jnp.where(live, pltpu.roll(pair, FOX_HEAD_DIM, axis=1), pad).astype(BF)

    bd = bd_ref[...]
    zq = _dot(hb, wa_ref[:, OFF_Q:OFF_K])
    qms = _dot((zq * zq).astype(BF), bd)
    split_heads(zq * lax.rsqrt(qms + EPS) * qg_ref[...], BIAS_TERMS, q_ref)
    zk = _dot(hb, wa_ref[:, OFF_K:OFF_V])
    kms = _dot((zk * zk).astype(BF), bd)
    kn = zk * lax.rsqrt(kms + EPS) * kg_ref[...]
    kT_ref[:, 0:FOX_HEAD_DIM, :] = kn.T.reshape(FOX_HEADS, FOX_HEAD_DIM, tm).astype(BF)
    kT_ref[:, FOX_HEAD_DIM:, :] = jnp.zeros((FOX_HEADS, LANES - FOX_HEAD_DIM, tm), BF)
    split_heads(_dot(hb, wa_ref[:, OFF_V:OFF_F]), 1, v_ref)

    zf = _dot(hb, wf_ref[...]) + bf_ref[...]
    logf = jnp.minimum(zf, 0.0) - jnp.log(1.0 + jnp.exp(-jnp.abs(zf)))
    lf_ref[...] = logf.T[0:FOX_HEADS, :]

    zm = _dot(hb, wa_ref[:, OFF_F:OFF_F + MEM_WIDTH])
    for h in range(MEM_HEADS):
        sl = slice(h * MEM_HEAD_DIM, (h + 1) * MEM_HEAD_DIM)
        qn = _rms(zm[:, sl], mqg_ref[...]).astype(BF)
        s = _dot(qn, mkT_ref[h])
        p = jnp.exp(s - jnp.max(s, axis=-1, keepdims=True))
        l = jnp.sum(p, axis=-1, keepdims=True)
        mo_ref[:, sl] = (_dot(p.astype(BF), mv_ref[:, sl]) / l).astype(BF)


def _in_proj(x2d, B, S, g, wa, wf, bfg, qg, kg, bd, mqg, mkT, mv):
    T, D = x2d.shape
    tm = TM_IN
    nk = S // tm
    wa_cols = wa.shape[1]
    M = mkT.shape[-1]
    tok = lambda i: (i, 0)
    return pl.pallas_call(
        _in_proj_kernel,
        grid=(T // tm,),
        in_specs=[pl.BlockSpec((tm, D), tok),
                  _resident((1, D)), _resident((D, wa_cols)), _resident((D, LANES)),
                  _resident((1, LANES)), _resident((1, FOX_WIDTH)), _resident((1, FOX_WIDTH)),
                  _resident((FOX_WIDTH, FOX_WIDTH)), _resident((1, MEM_HEAD_DIM)),
                  pl.BlockSpec((None, MEM_HEADS, MEM_HEAD_DIM, M), lambda i: (i // nk, 0, 0, 0)),
                  pl.BlockSpec((None, M, MEM_WIDTH), lambda i: (i // nk, 0, 0))],
        out_specs=[pl.BlockSpec((tm, POOL_WIDTH), tok),
                   pl.BlockSpec((None, FOX_HEADS, tm, LANES), lambda i: (i // nk, 0, i % nk, 0)),
                   pl.BlockSpec((None, FOX_HEADS, None, LANES, tm), lambda i: (i // nk, 0, i % nk, 0, 0)),
                   pl.BlockSpec((None, FOX_HEADS, tm, LANES), lambda i: (i // nk, 0, i % nk, 0)),
                   pl.BlockSpec((None, FOX_HEADS, tm), lambda i: (i // nk, 0, i % nk)),
                   pl.BlockSpec((tm, MEM_WIDTH), tok)],
        out_shape=[jax.ShapeDtypeStruct((T, POOL_WIDTH), BF),
                   jax.ShapeDtypeStruct((B, FOX_HEADS, S, LANES), BF),
                   jax.ShapeDtypeStruct((B, FOX_HEADS, nk, LANES, tm), BF),
                   jax.ShapeDtypeStruct((B, FOX_HEADS, S, LANES), BF),
                   jax.ShapeDtypeStruct((B, FOX_HEADS, S), F32),
                   jax.ShapeDtypeStruct((T, MEM_WIDTH), BF)],
        compiler_params=_params("arbitrary"),
        name="in_proj",
    )(x2d, g, wa, wf, bfg, qg, kg, bd, mqg, mkT, mv)


def _split_terms(x, n):
    terms = []
    for _ in range(n):
        t = x.astype(BF).astype(F32)
        terms.append(t)
        x = x - t
    return terms


def _cumsum_kernel(lf_ref, k_any_ref, krow_ref):
    del k_any_ref
    x = lf_ref[...]
    H, S = x.shape
    tk = krow_ref.shape[-1]
    lane = lax.broadcasted_iota(jnp.int32, x.shape, 1)
    k = 1
    while k < S:
        x = x + jnp.where(lane >= k, pltpu.roll(x, k, axis=1), 0.0)
        k *= 2
    terms = _split_terms(x * (-LOG2E), BIAS_TERMS)
    sub = lax.broadcasted_iota(jnp.int32, (BIAS_ROWS, tk), 0)
    ones_rows = jnp.where((sub >= BIAS_TERMS) & (sub < 2 * BIAS_TERMS), 1.0, 0.0)
    for j in range(S // tk):
        sl = slice(j * tk, (j + 1) * tk)
        for h in range(H):
            blk = ones_rows
            for r, t in enumerate(terms):
                blk = jnp.where(sub == r, t[h:h + 1, sl], blk)
            krow_ref[h, j] = blk.astype(BF)


def _cumsum(logf, kT):
    B, H, S = logf.shape
    nk, tk = kT.shape[2], kT.shape[4]
    assert FOX_HEAD_DIM % BIAS_ROWS == 0 and 2 * BIAS_TERMS <= BIAS_ROWS
    return pl.pallas_call(
        _cumsum_kernel, grid=(B,),
        in_specs=[pl.BlockSpec((None, H, S), lambda b: (b, 0, 0)),
                  pl.BlockSpec(memory_space=pl.ANY)],
        out_specs=pl.BlockSpec((None, H, nk, BIAS_ROWS, tk),
                               lambda b: (b, 0, 0, FOX_HEAD_DIM // BIAS_ROWS, 0)),
        out_shape=jax.ShapeDtypeStruct(kT.shape, kT.dtype),
        input_output_aliases={1: 0},
        compiler_params=_params("arbitrary"), name="forget_cumsum",
    )(logf, kT)


def _fox_kernel(exact_max, ub_ref, q_ref, kT_ref, v_ref, o_ref):
    tq = q_ref.shape[1]
    tk = kT_ref.shape[-1]
    i = pl.program_id(2)
    row = lax.broadcasted_iota(jnp.int32, (tq, tk), 0)
    col = lax.broadcasted_iota(jnp.int32, (tq, tk), 1)
    causal = col <= row
    lane = lax.broadcasted_iota(jnp.int32, (tq, LANES), 1)
    bias_lanes = (lane >= FOX_HEAD_DIM) & (lane < FOX_HEAD_DIM + BIAS_TERMS)

    def scores(q, h, j, diagonal):
        z = _dot(q, kT_ref[h, j])
        return jnp.where(causal, z, -jnp.inf) if diagonal else z

    def max_step(j, carry, diagonal):
        out = []
        for h in range(2):
            q = q_ref[h] if exact_max else jnp.where(bias_lanes, 1.0, 0.0).astype(BF)
            z = scores(q, h, j, diagonal)
            m = carry[h]
            for c in range(tk // LANES):
                m = jnp.maximum(m, z[:, c * LANES:(c + 1) * LANES])
            out.append(m)
        return tuple(out)

    m_lane = tuple(jnp.full((tq, LANES), -jnp.inf, F32) for _ in range(2))
    if exact_max:
        m_lane = lax.fori_loop(0, i, lambda j, c: max_step(j, c, False), m_lane)
    m_lane = max_step(i, m_lane, True)
    q_stab = []
    for h in range(2):
        m_row = jnp.max(m_lane[h], axis=-1, keepdims=True)
        if not exact_max:
            m_row = m_row + ub_ref[0]
        q = q_ref[h]
        for r, t in enumerate(_split_terms(-m_row, BIAS_TERMS)):
            q = jnp.where(lane == FOX_HEAD_DIM + BIAS_TERMS + r, t.astype(BF), q)
        q_stab.append(q)

    def pv_step(j, carry, diagonal):
        rows = pl.ds(pl.multiple_of(j * tk, tk), tk)
        out = []
        for h in range(2):
            p = jnp.exp2(scores(q_stab[h], h, j, diagonal))
            out.append(carry[h] + _dot(p.astype(BF), v_ref[h, rows, :]))
        return tuple(out)

    a_init = tuple(jnp.zeros((tq, LANES), F32) for _ in range(2))
    a0, a1 = pv_step(i, lax.fori_loop(0, i, lambda j, c: pv_step(j, c, False), a_init), True)
    o0 = a0 / a0[:, FOX_HEAD_DIM:FOX_HEAD_DIM + 1]
    o1 = a1 / a1[:, FOX_HEAD_DIM:FOX_HEAD_DIM + 1]
    o_ref[...] = jnp.where(lane < FOX_HEAD_DIM, o0, pltpu.roll(o1, FOX_HEAD_DIM, axis=1)).astype(BF)


def _fox_call(exact_max, ub, q, kT, v):
    B, H, S, _ = q.shape
    nk = S // TQ
    grid_spec = pltpu.PrefetchScalarGridSpec(
        num_scalar_prefetch=1,
        grid=(B, H // 2, nk),
        in_specs=[pl.BlockSpec((None, 2, TQ, LANES), lambda b, h, i, ub: (b, h, i, 0)),
                  pl.BlockSpec((None, 2, nk, LANES, TQ), lambda b, h, i, ub: (b, h, 0, 0, 0)),
                  pl.BlockSpec((None, 2, S, LANES), lambda b, h, i, ub: (b, h, 0, 0))],
        out_specs=pl.BlockSpec((None, TQ, LANES), lambda b, h, i, ub: (b, i, h)),
    )
    return pl.pallas_call(
        functools.partial(_fox_kernel, exact_max),
        grid_spec=grid_spec,
        out_shape=jax.ShapeDtypeStruct((B, S, H * FOX_HEAD_DIM), BF),
        compiler_params=_params("arbitrary", "arbitrary", "arbitrary"),
        name="fox_attention_exact_max" if exact_max else "fox_attention",
    )(ub, q, kT, v)


def _fox(score_bound, q, kT, v):
    ub = score_bound.reshape(1).astype(F32)
    return lax.cond(2.0 * score_bound <= SAFE_EXP2_SPAN,
                    functools.partial(_fox_call, False), functools.partial(_fox_call, True),
                    ub, q, kT, v)


def _merge_body(n_seq_tiles, x_ref, u_ref, up_ref, fox_ref, mo_ref, g_ref, wg_ref, pw_ref, ps_ref,
                wp_ref, wfb_ref, wmb_ref, wo_ref):
    tm = x_ref.shape[0]
    i = pl.program_id(0)
    x = x_ref[...]
    hb = _rms(x, g_ref[...]).astype(BF)

    seq_tile = i % n_seq_tiles
    halo = jnp.where(seq_tile == 0, 0.0, up_ref[...].astype(F32))
    u = u_ref[...].astype(F32)
    ext = jnp.concatenate([halo, u], axis=0)
    pos = seq_tile * tm + lax.broadcasted_iota(jnp.int32, (tm, 1), 0)
    outs = []
    for g, w in enumerate(POOL_WINDOWS):
        sl = slice(g * POOL_GROUP, (g + 1) * POOL_GROUP)
        s = ext[:, sl]
        k = 1
        while k < w:
            s = s + pltpu.roll(s, k, axis=0)
            k *= 2
        cnt = jnp.minimum(pos + 1, w).astype(F32)
        d = s[POOL_HALO:, :] / cnt - u[:, sl]
        outs.append(_dot(d.astype(BF), pw_ref[g]))
    pool_o = (jnp.concatenate(outs, axis=1) * ps_ref[...]).astype(BF)

    merged = _sigmoid(_dot(hb, wg_ref[:, 0:D_MODEL])) * _dot(pool_o, wp_ref[...])
    merged += _sigmoid(_dot(hb, wg_ref[:, D_MODEL:2 * D_MODEL])) * _dot(fox_ref[...], wfb_ref[...])
    merged += _sigmoid(_dot(hb, wg_ref[:, 2 * D_MODEL:])) * _dot(mo_ref[...], wmb_ref[...])
    return x + _dot(merged.astype(BF), wo_ref[...])


def _swiglu(hb, wgu_ref, wd_ref, act_ref, d_hidden):
    for c in range(d_hidden // FF_CHUNK):
        lo = c * FF_CHUNK
        g = _dot(hb, wgu_ref[:, lo:lo + FF_CHUNK])
        up = _dot(hb, wgu_ref[:, d_hidden + lo:d_hidden + lo + FF_CHUNK])
        act_ref[:, lo:lo + FF_CHUNK] = (g * _sigmoid(g) * up).astype(BF)
    return _dot(act_ref[...], wd_ref[...])


def _merge_ffn_kernel(n_seq_tiles, x_ref, u_ref, up_ref, fox_ref, mo_ref, g_ref, wg_ref, pw_ref,
                      ps_ref, wp_ref, wfb_ref, wmb_ref, wo_ref, g2_ref, wgu_ref, wd_ref,
                      o_ref, act_ref):
    x1 = _merge_body(n_seq_tiles, x_ref, u_ref, up_ref, fox_ref, mo_ref, g_ref, wg_ref, pw_ref,
                     ps_ref, wp_ref, wfb_ref, wmb_ref, wo_ref)
    h2 = _rms(x1, g2_ref[...]).astype(BF)
    o_ref[...] = x1 + _swiglu(h2, wgu_ref, wd_ref, act_ref, D_FF)


def _split_bf16(a):
    hi = a.astype(BF)
    return hi, (a - hi.astype(F32)).astype(BF)


ROW_TILE = D_MODEL // LANES


def _store_row_tiles(ref, x, row0=0, stride=ROW_TILE):
    n = x.shape[0]
    for c in range(ROW_TILE):
        ref[pl.ds(row0 + c, n, stride=stride), :] = x[:, c * LANES:(c + 1) * LANES]


def _load_row_tiles(ref, n, row0=0, stride=ROW_TILE):
    return jnp.concatenate([ref[pl.ds(row0 + c, n, stride=stride), :] for c in range(ROW_TILE)], axis=1)


def _merge_router_kernel(n_seq_tiles, x_ref, u_ref, up_ref, fox_ref, mo_ref, g_ref, wg_ref, pw_ref,
                         ps_ref, wp_ref, wfb_ref, wmb_ref, wo_ref, g2_ref, wr_ref, br_ref,
                         x1_ref, h2_ref, route_ref):
    x1 = _merge_body(n_seq_tiles, x_ref, u_ref, up_ref, fox_ref, mo_ref, g_ref, wg_ref, pw_ref,
                     ps_ref, wp_ref, wfb_ref, wmb_ref, wo_ref)
    x1_ref[...] = x1
    h2 = _rms(x1, g2_ref[...])
    _store_row_tiles(h2_ref, h2)

    h_hi, h_lo = _split_bf16(h2)
    w_hi, w_lo = _split_bf16(wr_ref[...])
    logits = _dot(h_hi, w_hi) + (_dot(h_hi, w_lo) + _dot(h_lo, w_hi)) + br_ref[...]
    lane = lax.broadcasted_iota(jnp.int32, logits.shape, 1)
    lane_f = lane.astype(F32)
    logits = jnp.where(lane < N_EXPERTS, logits, -jnp.inf)
    m1 = jnp.max(logits, axis=-1, keepdims=True)
    i1 = jnp.min(jnp.where(logits == m1, lane_f, float(LANES)), axis=-1, keepdims=True)
    rest = jnp.where(lane_f == i1, -jnp.inf, logits)
    m2 = jnp.max(rest, axis=-1, keepdims=True)
    i2 = jnp.min(jnp.where(rest == m2, lane_f, float(LANES)), axis=-1, keepdims=True)
    e = jnp.exp(m2 - m1)
    w1 = 1.0 / (1.0 + e)
    w2 = e / (1.0 + e)
    route = jnp.where(lane == 0, i1, 0.0)
    route = jnp.where(lane == 1, i2, route)
    route = jnp.where(lane == 2, w1, route)
    route_ref[...] = jnp.where(lane == 3, w2, route)


def _merge_specs(T, S, tm):
    D = D_MODEL
    tok = lambda i: (i, 0)
    halo_blocks = tm // POOL_HALO
    in_specs = [pl.BlockSpec((tm, D), tok),
                pl.BlockSpec((tm, POOL_WIDTH), tok),
                pl.BlockSpec((POOL_HALO, POOL_WIDTH), lambda i: (jnp.maximum(i * halo_blocks - 1, 0), 0)),
                pl.BlockSpec((tm, FOX_WIDTH), tok),
                pl.BlockSpec((tm, MEM_WIDTH), tok),
                _resident((1, D)), _resident((D, N_BRANCH * D)),
                _resident((len(POOL_WINDOWS), POOL_GROUP, POOL_GROUP)), _resident((1, POOL_WIDTH)),
                _resident((POOL_WIDTH, D)), _resident((FOX_WIDTH, D)), _resident((MEM_WIDTH, D)),
                _resident((D, D)), _resident((1, D))]
    return in_specs, tok


def _merge_ffn(x2d, S, u, fox_o, mem_o, g, wg, pw, ps, wp, wfb, wmb, wo, g2, wgu, wd):
    T, D = x2d.shape
    tm = TM_MERGE
    in_specs, tok = _merge_specs(T, S, tm)
    in_specs += [_resident((D, 2 * D_FF)), _resident((D_FF, D))]
    return pl.pallas_call(
        functools.partial(_merge_ffn_kernel, S // tm),
        grid=(T // tm,),
        in_specs=in_specs,
        out_specs=pl.BlockSpec((tm, D), tok),
        out_shape=jax.ShapeDtypeStruct((T, D), F32),
        scratch_shapes=[pltpu.VMEM((tm, D_FF), BF)],
        compiler_params=_params("arbitrary"),
        name="merge_ffn",
    )(x2d, u, u, fox_o, mem_o, g, wg, pw, ps, wp, wfb, wmb, wo, g2, wgu, wd)


def _merge_router(x2d, S, u, fox_o, mem_o, g, wg, pw, ps, wp, wfb, wmb, wo, g2, wr, br):
    T, D = x2d.shape
    tm = TM_MERGE
    in_specs, tok = _merge_specs(T, S, tm)
    in_specs += [_resident((D, LANES)), _resident((1, LANES))]
    return pl.pallas_call(
        functools.partial(_merge_router_kernel, S // tm),
        grid=(T // tm,),
        in_specs=in_specs,
        out_specs=[pl.BlockSpec((tm, D), tok), pl.BlockSpec((tm * ROW_TILE, LANES), tok),
                   pl.BlockSpec((tm, LANES), tok)],
        out_shape=[jax.ShapeDtypeStruct((T, D), F32), jax.ShapeDtypeStruct((T * ROW_TILE, LANES), F32),
                   jax.ShapeDtypeStruct((T, LANES), F32)],
        compiler_params=_params("arbitrary"),
        name="merge_router",
    )(x2d, u, u, fox_o, mem_o, g, wg, pw, ps, wp, wfb, wmb, wo, g2, wr, br)


def _row_tile(ref, r):
    start = r * ROW_TILE if isinstance(r, int) else pl.multiple_of(r * ROW_TILE, ROW_TILE)
    return ref.at[pl.ds(start, ROW_TILE)]


def _dispatch_kernel(pad_ref, pos_ref, zeros_ref, h_ref, xs_ref, sem):
    i = pl.program_id(0)
    n = pos_ref.shape[-1]
    pad_rows = zeros_ref.shape[0]

    def zero_fill(first_slot):
        dst = xs_ref.at[pl.ds(pl.multiple_of(first_slot * ROW_TILE, ROW_TILE), pad_rows)]
        fill = pltpu.make_async_copy(zeros_ref, dst, sem)
        fill.start()
        fill.wait()

    @pl.when(i == 0)
    def _():
        for e in range(N_EXPERTS):
            zero_fill(pad_ref[e])
        tile_slots = pad_rows // ROW_TILE

        def tail(t, carry):
            zero_fill(t * tile_slots)
            return carry

        lax.fori_loop(pad_ref[N_EXPERTS], xs_ref.shape[0] // pad_rows, tail, 0)

    def row_copy(a, slot):
        return pltpu.make_async_copy(_row_tile(h_ref, lax.div(a, TOP_K)), _row_tile(xs_ref, slot), sem)

    def issue(r, carry):
        row_copy(i * n + r, pos_ref[0, 0, r]).start()
        return carry

    def drain(r, carry):
        row_copy(0, 0).wait()
        return carry

    lax.fori_loop(0, n, issue, 0, unroll=8)
    lax.fori_loop(0, n, drain, 0, unroll=8)


def _dispatch(h_tiles, pos, pad_start, n_slots):
    n = pos.shape[0]
    R = DISPATCH_ROWS
    grid_spec = pltpu.PrefetchScalarGridSpec(
        num_scalar_prefetch=1,
        grid=(n // R,),
        in_specs=[pl.BlockSpec((1, 1, R), lambda i, pad: (i, 0, 0), memory_space=pltpu.SMEM),
                  pl.BlockSpec(memory_space=pl.ANY), pl.BlockSpec(memory_space=pl.ANY)],
        out_specs=pl.BlockSpec(memory_space=pl.ANY),
        scratch_shapes=[pltpu.SemaphoreType.DMA],
    )
    zeros = jnp.zeros((TM_EXPERT * ROW_TILE, LANES), F32)
    return pl.pallas_call(
        _dispatch_kernel,
        grid_spec=grid_spec,
        out_shape=jax.ShapeDtypeStruct((n_slots * ROW_TILE, LANES), F32),
        compiler_params=_params("arbitrary"),
        name="moe_dispatch",
    )(pad_start, pos.reshape(n // R, 1, R), zeros, h_tiles)


def _expert_kernel(te_ref, nu_ref, xs_ref, wgu_ref, wd_ref, y_ref, act_ref):
    i = pl.program_id(0)
    tm = act_ref.shape[0]

    @pl.when(i < nu_ref[0])
    def _():
        xb = _load_row_tiles(xs_ref, tm).astype(BF)
        _store_row_tiles(y_ref, _swiglu(xb, wgu_ref, wd_ref, act_ref, D_EXPERT))

    @pl.when(i >= nu_ref[0])
    def _():
        y_ref[...] = jnp.zeros_like(y_ref)


def _experts(xs, tile_expert, n_used, wgu, wd):
    D = D_MODEL
    tm = TM_EXPERT
    blk = tm * ROW_TILE
    used = lambda i, te, nu: (jnp.minimum(i, nu[0] - 1), 0)
    grid_spec = pltpu.PrefetchScalarGridSpec(
        num_scalar_prefetch=2,
        grid=(xs.shape[0] // blk,),
        in_specs=[pl.BlockSpec((blk, LANES), used),
                  pl.BlockSpec((None, D, 2 * D_EXPERT), lambda i, te, nu: (te[i], 0, 0)),
                  pl.BlockSpec((None, D_EXPERT, D), lambda i, te, nu: (te[i], 0, 0))],
        out_specs=pl.BlockSpec((blk, LANES), lambda i, te, nu: (i, 0)),
        scratch_shapes=[pltpu.VMEM((tm, D_EXPERT), BF)],
    )
    return pl.pallas_call(
        _expert_kernel,
        grid_spec=grid_spec,
        out_shape=jax.ShapeDtypeStruct(xs.shape, F32),
        compiler_params=_params("arbitrary"),
        name="experts",
    )(tile_expert, n_used, xs, wgu, wd)


def _combine_kernel(pos_ref, x1_ref, route_ref, y_ref, o_ref, yv_ref, sem):
    tm = x1_ref.shape[0]
    n = pos_ref.shape[-1]

    def row_copy(slot, r):
        return pltpu.make_async_copy(_row_tile(y_ref, slot), _row_tile(yv_ref, r), sem)

    def issue(r, carry):
        row_copy(pos_ref[0, 0, r], r).start()
        return carry

    def drain(r, carry):
        row_copy(0, 0).wait()
        return carry

    lax.fori_loop(0, n, issue, 0, unroll=8)
    lax.fori_loop(0, n, drain, 0, unroll=8)
    w = route_ref[...]
    y0 = _load_row_tiles(yv_ref, tm, 0, TOP_K * ROW_TILE)
    y1 = _load_row_tiles(yv_ref, tm, ROW_TILE, TOP_K * ROW_TILE)
    o_ref[...] = x1_ref[...] + w[:, 2:3] * y0 + w[:, 3:4] * y1


def _combine(x1, route, y_tiles, pos):
    T, D = x1.shape
    tm = TM_MERGE
    n = TOP_K * tm
    return pl.pallas_call(
        _combine_kernel,
        grid=(T // tm,),
        in_specs=[pl.BlockSpec((1, 1, n), lambda i: (i, 0, 0), memory_space=pltpu.SMEM),
                  pl.BlockSpec((tm, D), lambda i: (i, 0)),
                  pl.BlockSpec((tm, LANES), lambda i: (i, 0)),
                  pl.BlockSpec(memory_space=pl.ANY)],
        out_specs=pl.BlockSpec((tm, D), lambda i: (i, 0)),
        out_shape=jax.ShapeDtypeStruct((T, D), F32),
        scratch_shapes=[pltpu.VMEM((n * ROW_TILE, LANES), F32), pltpu.SemaphoreType.DMA],
        compiler_params=_params("arbitrary"),
        name="moe_combine",
    )(pos.reshape(T // tm, 1, n), x1, route, y_tiles)


def _dispatch_plan(route, T):
    tm = TM_EXPERT
    n_slots = TOP_K * T + (N_EXPERTS + 1) * tm
    n_tiles = n_slots // tm
    e = route[:, :TOP_K].astype(jnp.int32).reshape(-1)
    onehot = (e[:, None] == jnp.arange(N_EXPERTS, dtype=jnp.int32)[None, :]).astype(jnp.int32)
    csum = jnp.cumsum(onehot, axis=0)
    counts = csum[-1]
    rank = jnp.take_along_axis(csum, e[:, None], axis=1)[:, 0] - 1
    padded = ((counts + tm - 1) // tm) * tm
    ends = jnp.cumsum(padded)
    starts = ends - padded
    pos = (starts[e] + rank).astype(jnp.int32)
    tile_ends = ends // tm
    pad_start = jnp.concatenate([starts + counts, tile_ends[-1:]]).astype(jnp.int32)
    tile_ids = jnp.arange(n_tiles, dtype=jnp.int32)
    tile_expert = jnp.minimum(jnp.sum((tile_ids[:, None] >= tile_ends[None, :]).astype(jnp.int32), axis=1),
                              N_EXPERTS - 1)
    n_used = tile_ends[-1:].astype(jnp.int32)
    return pos, pad_start, tile_expert, n_used, n_slots


def _row(v, width=None):
    v = v.astype(F32).reshape(1, -1)
    if width is not None and v.shape[1] < width:
        v = jnp.pad(v, ((0, 0), (0, width - v.shape[1])))
    return v


def kernel(x, mem, mix_norm_g, w_in, b_forget, fox_q_g, fox_k_g, pool_w, pool_scale, mem_norm_g,
           w_mem_kv, mem_q_g, mem_k_g, w_pool_br, w_fox_br, w_mem_br, w_out, ffn_norm_g, w_ffn_gu,
           w_ffn_down, w_router, b_router, w_exp_gu, w_exp_down):
    B, S, D = x.shape
    T = B * S
    depth = w_in.shape[0]
    assert D == D_MODEL and S % TM_IN == 0 and S % TM_MERGE == 0 and mem.shape[1] == N_MEM
    nk = S // TQ
    head_mean = jnp.kron(jnp.eye(FOX_HEADS, dtype=F32),
                         jnp.full((FOX_HEAD_DIM, FOX_HEAD_DIM), 1.0 / FOX_HEAD_DIM, F32)).astype(BF)

    x2d = x.reshape(T, D)
    for layer in range(depth):
        w = w_in[layer]
        wa = jnp.concatenate([w[:, OFF_POOL:OFF_F], w[:, OFF_MQ:OFF_G]], axis=1).astype(BF)
        wf = jnp.pad(w[:, OFF_F:OFF_MQ], ((0, 0), (0, LANES - FOX_HEADS))).astype(BF)
        wg = w[:, OFF_G:].astype(BF)
        qg = _row(jnp.tile(fox_q_g[layer], FOX_HEADS) * (FOX_HEAD_DIM ** -0.5 * LOG2E))
        kg = _row(jnp.tile(fox_k_g[layer], FOX_HEADS))
        mqg = _row(mem_q_g[layer] * MEM_HEAD_DIM ** -0.5)

        mkT, mv = _mem_kv(mem, _row(mem_norm_g[layer]), w_mem_kv[layer].astype(BF), _row(mem_k_g[layer]))
        u, q, kT, v, logf, mem_o = _in_proj(
            x2d, B, S, _row(mix_norm_g[layer]), wa, wf, _row(b_forget[layer], LANES), qg, kg,
            head_mean, mqg, mkT, mv)
        score_bound = (NORM_SLACK * FOX_HEAD_DIM * jnp.max(jnp.abs(qg)) * jnp.max(jnp.abs(kg)))
        fox_o = _fox(score_bound, q, _cumsum(logf, kT), v).reshape(T, FOX_WIDTH)

        merge_args = (x2d, S, u, fox_o, mem_o, _row(mix_norm_g[layer]), wg, pool_w[layer].astype(BF),
                      _row(pool_scale[layer]), w_pool_br[layer].astype(BF), w_fox_br[layer].astype(BF),
                      w_mem_br[layer].astype(BF), w_out[layer].astype(BF), _row(ffn_norm_g[layer]))
        if layer % 2 == 0:
            x2d = _merge_ffn(*merge_args, w_ffn_gu[layer // 2].astype(BF), w_ffn_down[layer // 2].astype(BF))
        else:
            m = layer // 2
            wr = jnp.pad(w_router[m], ((0, 0), (0, LANES - N_EXPERTS)))
            x1, h2, route = _merge_router(*merge_args, wr, _row(b_router[m], LANES))
            pos, pad_start, tile_expert, n_used, n_slots = _dispatch_plan(route, T)
            xs = _dispatch(h2, pos, pad_start, n_slots)
            y = _experts(xs, tile_expert, n_used, w_exp_gu[m].astype(BF), w_exp_down[m].astype(BF))
            x2d = _combine(x1, route, y, pos)
    return x2d.reshape(B, S, D)
```

```python
import functools

import jax
import jax.numpy as jnp
from jax import lax
from jax.experimental import pallas as pl
from jax.experimental.pallas import tpu as pltpu

D_MODEL = 1024
N_MEM = 256
POOL_WINDOWS = (2, 4, 8, 16)
POOL_WIDTH = 512
POOL_GROUP = 128
POOL_HALO = 16
FOX_HEADS = 8
FOX_HEAD_DIM = 64
FOX_WIDTH = 512
MEM_HEADS = 4
MEM_HEAD_DIM = 128
MEM_WIDTH = 512
N_BRANCH = 3
D_FF = 2816
N_EXPERTS = 8
TOP_K = 2
D_EXPERT = 3584
EPS = 1e-6

OFF_POOL = 0
OFF_Q = OFF_POOL + POOL_WIDTH
OFF_K = OFF_Q + FOX_WIDTH
OFF_V = OFF_K + FOX_WIDTH
OFF_F = OFF_V + FOX_WIDTH
OFF_MQ = OFF_F + FOX_HEADS
OFF_G = OFF_MQ + MEM_WIDTH

LANES = 128
LOG2E = 1.4426950408889634
BIAS_TERMS = 3
BIAS_ROWS = 16
SAFE_EXP2_SPAN = 100.0
NORM_SLACK = 1.02
BF = jnp.bfloat16
F32 = jnp.float32

TM_IN = 512
TQ = 512
TM_MERGE = 256
TM_EXPERT = 256
DISPATCH_ROWS = 512
FF_CHUNK = 256
VMEM_LIMIT = 58 * 1024 * 1024


def _resident(shape):
    nd = len(shape)
    return pl.BlockSpec(shape, lambda *_: (0,) * nd, pipeline_mode=pl.Buffered(1))


def _params(*sem):
    return pltpu.CompilerParams(dimension_semantics=sem, vmem_limit_bytes=VMEM_LIMIT)


def _dot(a, b):
    return jnp.dot(a, b, preferred_element_type=F32)


def _rms(x, g):
    ms = jnp.mean(x * x, axis=-1, keepdims=True)
    return x * lax.rsqrt(ms + EPS) * g


def _sigmoid(z):
    return 1.0 / (1.0 + jnp.exp(-z))


def _mem_kv_kernel(mem_ref, g_ref, w_ref, kg_ref, kT_ref, v_ref):
    mn = _rms(mem_ref[...], g_ref[...]).astype(BF)
    kv = _dot(mn, w_ref[...])
    for h in range(MEM_HEADS):
        kh = kv[:, h * MEM_HEAD_DIM:(h + 1) * MEM_HEAD_DIM]
        kT_ref[h] = _rms(kh, kg_ref[...]).T.astype(BF)
    v_ref[...] = kv[:, MEM_WIDTH:].astype(BF)


def _mem_kv(mem, g, w_kv, k_g):
    B, M, D = mem.shape
    return pl.pallas_call(
        _mem_kv_kernel,
        grid=(B,),
        in_specs=[pl.BlockSpec((None, M, D), lambda b: (b, 0, 0)),
                  _resident((1, D)), _resident((D, 2 * MEM_WIDTH)), _resident((1, MEM_HEAD_DIM))],
        out_specs=[pl.BlockSpec((None, MEM_HEADS, MEM_HEAD_DIM, M), lambda b: (b, 0, 0, 0)),
                   pl.BlockSpec((None, M, MEM_WIDTH), lambda b: (b, 0, 0))],
        out_shape=[jax.ShapeDtypeStruct((B, MEM_HEADS, MEM_HEAD_DIM, M), BF),
                   jax.ShapeDtypeStruct((B, M, MEM_WIDTH), BF)],
        compiler_params=_params("arbitrary"),
        name="mem_kv",
    )(mem, g, w_kv, k_g)


def _in_proj_kernel(x_ref, g_ref, wa_ref, wf_ref, bf_ref, qg_ref, kg_ref, bd_ref, mqg_ref,
                    mkT_ref, mv_ref, u_ref, q_ref, kT_ref, v_ref, lf_ref, mo_ref):
    tm = x_ref.shape[0]
    hb = _rms(x_ref[...], g_ref[...]).astype(BF)

    u_ref[...] = _dot(hb, wa_ref[:, OFF_POOL:OFF_Q]).astype(BF)

    lane = lax.broadcasted_iota(jnp.int32, (tm, LANES), 1)
    live = lane < FOX_HEAD_DIM

    def split_heads(z, n_ones, ref):
        pad = jnp.where(lane < FOX_HEAD_DIM + n_ones, 1.0, 0.0)
        for p in range(FOX_HEADS // 2):
            pair = z[:, p * LANES:(p + 1) * LANES]
            ref[2 * p] = jnp.where(live, pair, pad).astype(BF)
            ref[2 * p + 1] = jnp.where(live, pltpu.roll(pair, FOX_HEAD_DIM, axis=1), pad).astype(BF)

    bd = bd_ref[...]
    zq = _dot(hb, wa_ref[:, OFF_Q:OFF_K])
    qms = _dot((zq * zq).astype(BF), bd)
    split_heads(zq * lax.rsqrt(qms + EPS) * qg_ref[...], BIAS_TERMS, q_ref)
    zk = _dot(hb, wa_ref[:, OFF_K:OFF_V])
    kms = _dot((zk * zk).astype(BF), bd)
    kn = zk * lax.rsqrt(kms + EPS) * kg_ref[...]
    kT_ref[:, 0:FOX_HEAD_DIM, :] = kn.T.reshape(FOX_HEADS, FOX_HEAD_DIM, tm).astype(BF)
    kT_ref[:, FOX_HEAD_DIM:, :] = jnp.zeros((FOX_HEADS, LANES - FOX_HEAD_DIM, tm), BF)
    split_heads(_dot(hb, wa_ref[:, OFF_V:OFF_F]), 1, v_ref)

    zf = _dot(hb, wf_ref[...]) + bf_ref[...]
    logf = jnp.minimum(zf, 0.0) - jnp.log(1.0 + jnp.exp(-jnp.abs(zf)))
    lf_ref[...] = logf.T[0:FOX_HEADS, :]

    zm = _dot(hb, wa_ref[:, OFF_F:OFF_F + MEM_WIDTH])
    for h in range(MEM_HEADS):
        sl = slice(h * MEM_HEAD_DIM, (h + 1) * MEM_HEAD_DIM)
        qn = _rms(zm[:, sl], mqg_ref[...]).astype(BF)
        s = _dot(qn, mkT_ref[h])
        p = jnp.exp(s - jnp.max(s, axis=-1, keepdims=True))
        l = jnp.sum(p, axis=-1, keepdims=True)
        mo_ref[:, sl] = (_dot(p.astype(BF), mv_ref[:, sl]) / l).astype(BF)


def _in_proj(x2d, B, S, g, wa, wf, bfg, qg, kg, bd, mqg, mkT, mv):
    T, D = x2d.shape
    tm = TM_IN
    nk = S // tm
    wa_cols = wa.shape[1]
    M = mkT.shape[-1]
    tok = lambda i: (i, 0)
    return pl.pallas_call(
        _in_proj_kernel,
        grid=(T // tm,),
        in_specs=[pl.BlockSpec((tm, D), tok),
                  _resident((1, D)), _resident((D, wa_cols)), _resident((D, LANES)),
                  _resident((1, LANES)), _resident((1, FOX_WIDTH)), _resident((1, FOX_WIDTH)),
                  _resident((FOX_WIDTH, FOX_WIDTH)), _resident((1, MEM_HEAD_DIM)),
                  pl.BlockSpec((None, MEM_HEADS, MEM_HEAD_DIM, M), lambda i: (i // nk, 0, 0, 0)),
                  pl.BlockSpec((None, M, MEM_WIDTH), lambda i: (i // nk, 0, 0))],
        out_specs=[pl.BlockSpec((tm, POOL_WIDTH), tok),
                   pl.BlockSpec((None, FOX_HEADS, tm, LANES), lambda i: (i // nk, 0, i % nk, 0)),
                   pl.BlockSpec((None, FOX_HEADS, None, LANES, tm), lambda i: (i // nk, 0, i % nk, 0, 0)),
                   pl.BlockSpec((None, FOX_HEADS, tm, LANES), lambda i: (i // nk, 0, i % nk, 0)),
                   pl.BlockSpec((None, FOX_HEADS, tm), lambda i: (i // nk, 0, i % nk)),
                   pl.BlockSpec((tm, MEM_WIDTH), tok)],
        out_shape=[jax.ShapeDtypeStruct((T, POOL_WIDTH), BF),
                   jax.ShapeDtypeStruct((B, FOX_HEADS, S, LANES), BF),
                   jax.ShapeDtypeStruct((B, FOX_HEADS, nk, LANES, tm), BF),
                   jax.ShapeDtypeStruct((B, FOX_HEADS, S, LANES), BF),
                   jax.ShapeDtypeStruct((B, FOX_HEADS, S), F32),
                   jax.ShapeDtypeStruct((T, MEM_WIDTH), BF)],
        compiler_params=_params("arbitrary"),
        name="in_proj",
    )(x2d, g, wa, wf, bfg, qg, kg, bd, mqg, mkT, mv)


def _split_terms(x, n):
    terms = []
    for _ in range(n):
        t = x.astype(BF).astype(F32)
        terms.append(t)
        x = x - t
    return terms


def _cumsum_kernel(lf_ref, k_any_ref, krow_ref):
    del k_any_ref
    x = lf_ref[...]
    H, S = x.shape
    tk = krow_ref.shape[-1]
    lane = lax.broadcasted_iota(jnp.int32, x.shape, 1)
    k = 1
    while k < S:
        x = x + jnp.where(lane >= k, pltpu.roll(x, k, axis=1), 0.0)
        k *= 2
    terms = _split_terms(x * (-LOG2E), BIAS_TERMS)
    sub = lax.broadcasted_iota(jnp.int32, (BIAS_ROWS, tk), 0)
    ones_rows = jnp.where((sub >= BIAS_TERMS) & (sub < 2 * BIAS_TERMS), 1.0, 0.0)
    for j in range(S // tk):
        sl = slice(j * tk, (j + 1) * tk)
        for h in range(H):
            blk = ones_rows
            for r, t in enumerate(terms):
                blk = jnp.where(sub == r, t[h:h + 1, sl], blk)
            krow_ref[h, j] = blk.astype(BF)


def _cumsum(logf, kT):
    B, H, S = logf.shape
    nk, tk = kT.shape[2], kT.shape[4]
    assert FOX_HEAD_DIM % BIAS_ROWS == 0 and 2 * BIAS_TERMS <= BIAS_ROWS
    return pl.pallas_call(
        _cumsum_kernel, grid=(B,),
        in_specs=[pl.BlockSpec((None, H, S), lambda b: (b, 0, 0)),
                  pl.BlockSpec(memory_space=pl.ANY)],
        out_specs=pl.BlockSpec((None, H, nk, BIAS_ROWS, tk),
                               lambda b: (b, 0, 0, FOX_HEAD_DIM // BIAS_ROWS, 0)),
        out_shape=jax.ShapeDtypeStruct(kT.shape, kT.dtype),
        input_output_aliases={1: 0},
        compiler_params=_params("arbitrary"), name="forget_cumsum",
    )(logf, kT)


def _fox_kernel(exact_max, ub_ref, q_ref, kT_ref, v_ref, o_ref):
    tq = q_ref.shape[1]
    tk = kT_ref.shape[-1]
    i = pl.program_id(2)
    row = lax.broadcasted_iota(jnp.int32, (tq, tk), 0)
    col = lax.broadcasted_iota(jnp.int32, (tq, tk), 1)
    causal = col <= row
    lane = lax.broadcasted_iota(jnp.int32, (tq, LANES), 1)
    bias_lanes = (lane >= FOX_HEAD_DIM) & (lane < FOX_HEAD_DIM + BIAS_TERMS)

    def scores(q, h, j, diagonal):
        z = _dot(q, kT_ref[h, j])
        return jnp.where(causal, z, -jnp.inf) if diagonal else z

    def max_step(j, carry, diagonal):
        out = []
        for h in range(2):
            q = q_ref[h] if exact_max else jnp.where(bias_lanes, 1.0, 0.0).astype(BF)
            z = scores(q, h, j, diagonal)
            m = carry[h]
            for c in range(tk // LANES):
                m = jnp.maximum(m, z[:, c * LANES:(c + 1) * LANES])
            out.append(m)
        return tuple(out)

    m_lane = tuple(jnp.full((tq, LANES), -jnp.inf, F32) for _ in range(2))
    if exact_max:
        m_lane = lax.fori_loop(0, i, lambda j, c: max_step(j, c, False), m_lane)
    m_lane = max_step(i, m_lane, True)
    q_stab = []
    for h in range(2):
        m_row = jnp.max(m_lane[h], axis=-1, keepdims=True)
        if not exact_max:
            m_row = m_row + ub_ref[0]
        q = q_ref[h]
        for r, t in enumerate(_split_terms(-m_row, BIAS_TERMS)):
            q = jnp.where(lane == FOX_HEAD_DIM + BIAS_TERMS + r, t.astype(BF), q)
        q_stab.append(q)

    def pv_step(j, carry, diagonal):
        rows = pl.ds(pl.multiple_of(j * tk, tk), tk)
        out = []
        for h in range(2):
            p = jnp.exp2(scores(q_stab[h], h, j, diagonal))
            out.append(carry[h] + _dot(p.astype(BF), v_ref[h, rows, :]))
        return tuple(out)

    a_init = tuple(jnp.zeros((tq, LANES), F32) for _ in range(2))
    a0, a1 = pv_step(i, lax.fori_loop(0, i, lambda j, c: pv_step(j, c, False), a_init), True)
    o0 = a0 / a0[:, FOX_HEAD_DIM:FOX_HEAD_DIM + 1]
    o1 = a1 / a1[:, FOX_HEAD_DIM:FOX_HEAD_DIM + 1]
    o_ref[...] = jnp.where(lane < FOX_HEAD_DIM, o0, pltpu.roll(o1, FOX_HEAD_DIM, axis=1)).astype(BF)


def _fox_call(exact_max, ub, q, kT, v):
    B, H, S, _ = q.shape
    nk = S // TQ
    grid_spec = pltpu.PrefetchScalarGridSpec(
        num_scalar_prefetch=1,
        grid=(B, H // 2, nk),
        in_specs=[pl.BlockSpec((None, 2, TQ, LANES), lambda b, h, i, ub: (b, h, i, 0)),
                  pl.BlockSpec((None, 2, nk, LANES, TQ), lambda b, h, i, ub: (b, h, 0, 0, 0)),
                  pl.BlockSpec((None, 2, S, LANES), lambda b, h, i, ub: (b, h, 0, 0))],
        out_specs=pl.BlockSpec((None, TQ, LANES), lambda b, h, i, ub: (b, i, h)),
    )
    return pl.pallas_call(
        functools.partial(_fox_kernel, exact_max),
        grid_spec=grid_spec,
        out_shape=jax.ShapeDtypeStruct((B, S, H * FOX_HEAD_DIM), BF),
        compiler_params=_params("arbitrary", "arbitrary", "arbitrary"),
        name="fox_attention_exact_max" if exact_max else "fox_attention",
    )(ub, q, kT, v)


def _fox(score_bound, q, kT, v):
    ub = score_bound.reshape(1).astype(F32)
    return lax.cond(2.0 * score_bound <= SAFE_EXP2_SPAN,
                    functools.partial(_fox_call, False), functools.partial(_fox_call, True),
                    ub, q, kT, v)


def _merge_body(n_seq_tiles, x_ref, u_ref, up_ref, fox_ref, mo_ref, g_ref, wg_ref, pw_ref, ps_ref,
                wp_ref, wfb_ref, wmb_ref, wo_ref):
    tm = x_ref.shape[0]
    i = pl.program_id(0)
    x = x_ref[...]
    hb = _rms(x, g_ref[...]).astype(BF)

    seq_tile = i % n_seq_tiles
    halo = jnp.where(seq_tile == 0, 0.0, up_ref[...].astype(F32))
    u = u_ref[...].astype(F32)
    ext = jnp.concatenate([halo, u], axis=0)
    pos = seq_tile * tm + lax.broadcasted_iota(jnp.int32, (tm, 1), 0)
    outs = []
    for g, w in enumerate(POOL_WINDOWS):
        sl = slice(g * POOL_GROUP, (g + 1) * POOL_GROUP)
        s = ext[:, sl]
        k = 1
        while k < w:
            s = s + pltpu.roll(s, k, axis=0)
            k *= 2
        cnt = jnp.minimum(pos + 1, w).astype(F32)
        d = s[POOL_HALO:, :] / cnt - u[:, sl]
        outs.append(_dot(d.astype(BF), pw_ref[g]))
    pool_o = (jnp.concatenate(outs, axis=1) * ps_ref[...]).astype(BF)

    merged = _sigmoid(_dot(hb, wg_ref[:, 0:D_MODEL])) * _dot(pool_o, wp_ref[...])
    merged += _sigmoid(_dot(hb, wg_ref[:, D_MODEL:2 * D_MODEL])) * _dot(fox_ref[...], wfb_ref[...])
    merged += _sigmoid(_dot(hb, wg_ref[:, 2 * D_MODEL:])) * _dot(mo_ref[...], wmb_ref[...])
    return x + _dot(merged.astype(BF), wo_ref[...])


def _swiglu(hb, wgu_ref, wd_ref, act_ref, d_hidden):
    for c in range(d_hidden // FF_CHUNK):
        lo = c * FF_CHUNK
        g = _dot(hb, wgu_ref[:, lo:lo + FF_CHUNK])
        up = _dot(hb, wgu_ref[:, d_hidden + lo:d_hidden + lo + FF_CHUNK])
        act_ref[:, lo:lo + FF_CHUNK] = (g * _sigmoid(g) * up).astype(BF)
    return _dot(act_ref[...], wd_ref[...])


def _merge_ffn_kernel(n_seq_tiles, x_ref, u_ref, up_ref, fox_ref, mo_ref, g_ref, wg_ref, pw_ref,
                      ps_ref, wp_ref, wfb_ref, wmb_ref, wo_ref, g2_ref, wgu_ref, wd_ref,
                      o_ref, act_ref):
    x1 = _merge_body(n_seq_tiles, x_ref, u_ref, up_ref, fox_ref, mo_ref, g_ref, wg_ref, pw_ref,
                     ps_ref, wp_ref, wfb_ref, wmb_ref, wo_ref)
    h2 = _rms(x1, g2_ref[...]).astype(BF)
    o_ref[...] = x1 + _swiglu(h2, wgu_ref, wd_ref, act_ref, D_FF)


def _split_bf16(a):
    hi = a.astype(BF)
    return hi, (a - hi.astype(F32)).astype(BF)


ROW_TILE = D_MODEL // LANES


def _store_row_tiles(ref, x, row0=0, stride=ROW_TILE):
    n = x.shape[0]
    for c in range(ROW_TILE):
        ref[pl.ds(row0 + c, n, stride=stride), :] = x[:, c * LANES:(c + 1) * LANES]


def _load_row_tiles(ref, n, row0=0, stride=ROW_TILE):
    return jnp.concatenate([ref[pl.ds(row0 + c, n, stride=stride), :] for c in range(ROW_TILE)], axis=1)


def _merge_router_kernel(n_seq_tiles, x_ref, u_ref, up_ref, fox_ref, mo_ref, g_ref, wg_ref, pw_ref,
                         ps_ref, wp_ref, wfb_ref, wmb_ref, wo_ref, g2_ref, wr_ref, br_ref,
                         x1_ref, h2_ref, route_ref):
    x1 = _merge_body(n_seq_tiles, x_ref, u_ref, up_ref, fox_ref, mo_ref, g_ref, wg_ref, pw_ref,
                     ps_ref, wp_ref, wfb_ref, wmb_ref, wo_ref)
    x1_ref[...] = x1
    h2 = _rms(x1, g2_ref[...])
    _store_row_tiles(h2_ref, h2)

    h_hi, h_lo = _split_bf16(h2)
    w_hi, w_lo = _split_bf16(wr_ref[...])
    logits = _dot(h_hi, w_hi) + (_dot(h_hi, w_lo) + _dot(h_lo, w_hi)) + br_ref[...]
    lane = lax.broadcasted_iota(jnp.int32, logits.shape, 1)
    lane_f = lane.astype(F32)
    logits = jnp.where(lane < N_EXPERTS, logits, -jnp.inf)
    m1 = jnp.max(logits, axis=-1, keepdims=True)
    i1 = jnp.min(jnp.where(logits == m1, lane_f, float(LANES)), axis=-1, keepdims=True)
    rest = jnp.where(lane_f == i1, -jnp.inf, logits)
    m2 = jnp.max(rest, axis=-1, keepdims=True)
    i2 = jnp.min(jnp.where(rest == m2, lane_f, float(LANES)), axis=-1, keepdims=True)
    e = jnp.exp(m2 - m1)
    w1 = 1.0 / (1.0 + e)
    w2 = e / (1.0 + e)
    route = jnp.where(lane == 0, i1, 0.0)
    route = jnp.where(lane == 1, i2, route)
    route = jnp.where(lane == 2, w1, route)
    route_ref[...] = jnp.where(lane == 3, w2, route)


def _merge_specs(T, S, tm):
    D = D_MODEL
    tok = lambda i: (i, 0)
    halo_blocks = tm // POOL_HALO
    in_specs = [pl.BlockSpec((tm, D), tok),
                pl.BlockSpec((tm, POOL_WIDTH), tok),
                pl.BlockSpec((POOL_HALO, POOL_WIDTH), lambda i: (jnp.maximum(i * halo_blocks - 1, 0), 0)),
                pl.BlockSpec((tm, FOX_WIDTH), tok),
                pl.BlockSpec((tm, MEM_WIDTH), tok),
                _resident((1, D)), _resident((D, N_BRANCH * D)),
                _resident((len(POOL_WINDOWS), POOL_GROUP, POOL_GROUP)), _resident((1, POOL_WIDTH)),
                _resident((POOL_WIDTH, D)), _resident((FOX_WIDTH, D)), _resident((MEM_WIDTH, D)),
                _resident((D, D)), _resident((1, D))]
    return in_specs, tok


def _merge_ffn(x2d, S, u, fox_o, mem_o, g, wg, pw, ps, wp, wfb, wmb, wo, g2, wgu, wd):
    T, D = x2d.shape
    tm = TM_MERGE
    in_specs, tok = _merge_specs(T, S, tm)
    in_specs += [_resident((D, 2 * D_FF)), _resident((D_FF, D))]
    return pl.pallas_call(
        functools.partial(_merge_ffn_kernel, S // tm),
        grid=(T // tm,),
        in_specs=in_specs,
        out_specs=pl.BlockSpec((tm, D), tok),
        out_shape=jax.ShapeDtypeStruct((T, D), F32),
        scratch_shapes=[pltpu.VMEM((tm, D_FF), BF)],
        compiler_params=_params("arbitrary"),
        name="merge_ffn",
    )(x2d, u, u, fox_o, mem_o, g, wg, pw, ps, wp, wfb, wmb, wo, g2, wgu, wd)


def _merge_router(x2d, S, u, fox_o, mem_o, g, wg, pw, ps, wp, wfb, wmb, wo, g2, wr, br):
    T, D = x2d.shape
    tm = TM_MERGE
    in_specs, tok = _merge_specs(T, S, tm)
    in_specs += [_resident((D, LANES)), _resident((1, LANES))]
    return pl.pallas_call(
        functools.partial(_merge_router_kernel, S // tm),
        grid=(T // tm,),
        in_specs=in_specs,
        out_specs=[pl.BlockSpec((tm, D), tok), pl.BlockSpec((tm * ROW_TILE, LANES), tok),
                   pl.BlockSpec((tm, LANES), tok)],
        out_shape=[jax.ShapeDtypeStruct((T, D), F32), jax.ShapeDtypeStruct((T * ROW_TILE, LANES), F32),
                   jax.ShapeDtypeStruct((T, LANES), F32)],
        compiler_params=_params("arbitrary"),
        name="merge_router",
    )(x2d, u, u, fox_o, mem_o, g, wg, pw, ps, wp, wfb, wmb, wo, g2, wr, br)


def _row_tile(ref, r):
    start = r * ROW_TILE if isinstance(r, int) else pl.multiple_of(r * ROW_TILE, ROW_TILE)
    return ref.at[pl.ds(start, ROW_TILE)]


def _dispatch_kernel(pad_ref, pos_ref, zeros_ref, h_ref, xs_ref, sem):
    i = pl.program_id(0)
    n = pos_ref.shape[-1]
    pad_rows = zeros_ref.shape[0]

    def zero_fill(first_slot):
        dst = xs_ref.at[pl.ds(pl.multiple_of(first_slot * ROW_TILE, ROW_TILE), pad_rows)]
        fill = pltpu.make_async_copy(zeros_ref, dst, sem)
        fill.start()
        fill.wait()

    @pl.when(i == 0)
    def _():
        for e in range(N_EXPERTS):
            zero_fill(pad_ref[e])
        tile_slots = pad_rows // ROW_TILE

        def tail(t, carry):
            zero_fill(t * tile_slots)
            return carry

        lax.fori_loop(pad_ref[N_EXPERTS], xs_ref.shape[0] // pad_rows, tail, 0)

    def row_copy(t, slot):
        return pltpu.make_async_copy(_row_tile(h_ref, t), _row_tile(xs_ref, slot), sem)

    def issue(t, carry):
        for k in range(TOP_K):
            row_copy(t, pos_ref[0, 0, TOP_K * t + k]).start(priority=k % 2)
        return carry

    def drain(r, carry):
        row_copy(0, 0).wait()
        return carry

    lax.fori_loop(0, n // TOP_K, issue, 0, unroll=4)
    lax.fori_loop(0, n, drain, 0, unroll=8)


def _dispatch(h_tiles, pos, pad_start, n_slots):
    n = pos.shape[0]
    R = DISPATCH_ROWS
    grid_spec = pltpu.PrefetchScalarGridSpec(
        num_scalar_prefetch=1,
        grid=(n // R,),
        in_specs=[pl.BlockSpec((1, 1, R), lambda i, pad: (i, 0, 0), memory_space=pltpu.SMEM),
                  pl.BlockSpec(memory_space=pl.ANY),
                  pl.BlockSpec((R // TOP_K * ROW_TILE, LANES), lambda i, pad: (i, 0))],
        out_specs=pl.BlockSpec(memory_space=pl.ANY),
        scratch_shapes=[pltpu.SemaphoreType.DMA],
    )
    zeros = jnp.zeros((TM_EXPERT * ROW_TILE, LANES), F32)
    return pl.pallas_call(
        _dispatch_kernel,
        grid_spec=grid_spec,
        out_shape=jax.ShapeDtypeStruct((n_slots * ROW_TILE, LANES), F32),
        compiler_params=_params("arbitrary"),
        name="moe_dispatch",
    )(pad_start, pos.reshape(n // R, 1, R), zeros, h_tiles)


def _expert_kernel(te_ref, nu_ref, xs_ref, wgu_ref, wd_ref, y_ref, act_ref):
    i = pl.program_id(0)
    tm = act_ref.shape[0]

    @pl.when(i < nu_ref[0])
    def _():
        xb = _load_row_tiles(xs_ref, tm).astype(BF)
        _store_row_tiles(y_ref, _swiglu(xb, wgu_ref, wd_ref, act_ref, D_EXPERT))

    @pl.when(i >= nu_ref[0])
    def _():
        y_ref[...] = jnp.zeros_like(y_ref)


def _experts(xs, tile_expert, n_used, wgu, wd):
    D = D_MODEL
    tm = TM_EXPERT
    blk = tm * ROW_TILE
    used = lambda i, te, nu: (jnp.minimum(i, nu[0] - 1), 0)
    grid_spec = pltpu.PrefetchScalarGridSpec(
        num_scalar_prefetch=2,
        grid=(xs.shape[0] // blk,),
        in_specs=[pl.BlockSpec((blk, LANES), used),
                  pl.BlockSpec((None, D, 2 * D_EXPERT), lambda i, te, nu: (te[i], 0, 0)),
                  pl.BlockSpec((None, D_EXPERT, D), lambda i, te, nu: (te[i], 0, 0))],
        out_specs=pl.BlockSpec((blk, LANES), lambda i, te, nu: (i, 0)),
        scratch_shapes=[pltpu.VMEM((tm, D_EXPERT), BF)],
    )
    return pl.pallas_call(
        _expert_kernel,
        grid_spec=grid_spec,
        out_shape=jax.ShapeDtypeStruct(xs.shape, F32),
        compiler_params=_params("arbitrary"),
        name="experts",
    )(tile_expert, n_used, xs, wgu, wd)


def _combine_kernel(pos_ref, x1_ref, route_ref, y_ref, o_ref, yv_ref, sem):
    tm = x1_ref.shape[0]
    n = pos_ref.shape[-1]

    def row_copy(slot, r):
        return pltpu.make_async_copy(_row_tile(y_ref, slot), _row_tile(yv_ref, r), sem)

    def issue(t, carry):
        for k in range(TOP_K):
            r = TOP_K * t + k
            row_copy(pos_ref[0, 0, r], r).start(priority=k % 2)
        return carry

    def drain(r, carry):
        row_copy(0, 0).wait()
        return carry

    lax.fori_loop(0, n // TOP_K, issue, 0, unroll=4)
    lax.fori_loop(0, n, drain, 0, unroll=8)
    w = route_ref[...]
    y0 = _load_row_tiles(yv_ref, tm, 0, TOP_K * ROW_TILE)
    y1 = _load_row_tiles(yv_ref, tm, ROW_TILE, TOP_K * ROW_TILE)
    o_ref[...] = x1_ref[...] + w[:, 2:3] * y0 + w[:, 3:4] * y1


def _combine(x1, route, y_tiles, pos):
    T, D = x1.shape
    tm = TM_MERGE
    n = TOP_K * tm
    return pl.pallas_call(
        _combine_kernel,
        grid=(T // tm,),
        in_specs=[pl.BlockSpec((1, 1, n), lambda i: (i, 0, 0), memory_space=pltpu.SMEM),
                  pl.BlockSpec((tm, D), lambda i: (i, 0)),
                  pl.BlockSpec((tm, LANES), lambda i: (i, 0)),
                  pl.BlockSpec(memory_space=pl.ANY)],
        out_specs=pl.BlockSpec((tm, D), lambda i: (i, 0)),
        out_shape=jax.ShapeDtypeStruct((T, D), F32),
        scratch_shapes=[pltpu.VMEM((n * ROW_TILE, LANES), F32), pltpu.SemaphoreType.DMA],
        compiler_params=_params("arbitrary"),
        name="moe_combine",
    )(pos.reshape(T // tm, 1, n), x1, route, y_tiles)


def _dispatch_plan(route, T):
    tm = TM_EXPERT
    n_slots = TOP_K * T + (N_EXPERTS + 1) * tm
    n_tiles = n_slots // tm
    e = route[:, :TOP_K].astype(jnp.int32).reshape(-1)
    onehot = (e[:, None] == jnp.arange(N_EXPERTS, dtype=jnp.int32)[None, :]).astype(jnp.int32)
    csum = jnp.cumsum(onehot, axis=0)
    counts = csum[-1]
    rank = jnp.take_along_axis(csum, e[:, None], axis=1)[:, 0] - 1
    padded = ((counts + tm - 1) // tm) * tm
    ends = jnp.cumsum(padded)
    starts = ends - padded
    pos = (starts[e] + rank).astype(jnp.int32)
    tile_ends = ends // tm
    pad_start = jnp.concatenate([starts + counts, tile_ends[-1:]]).astype(jnp.int32)
    tile_ids = jnp.arange(n_tiles, dtype=jnp.int32)
    tile_expert = jnp.minimum(jnp.sum((tile_ids[:, None] >= tile_ends[None, :]).astype(jnp.int32), axis=1),
                              N_EXPERTS - 1)
    n_used = tile_ends[-1:].astype(jnp.int32)
    return pos, pad_start, tile_expert, n_used, n_slots


def _row(v, width=None):
    v = v.astype(F32).reshape(1, -1)
    if width is not None and v.shape[1] < width:
        v = jnp.pad(v, ((0, 0), (0, width - v.shape[1])))
    return v


def kernel(x, mem, mix_norm_g, w_in, b_forget, fox_q_g, fox_k_g, pool_w, pool_scale, mem_norm_g,
           w_mem_kv, mem_q_g, mem_k_g, w_pool_br, w_fox_br, w_mem_br, w_out, ffn_norm_g, w_ffn_gu,
           w_ffn_down, w_router, b_router, w_exp_gu, w_exp_down):
    B, S, D = x.shape
    T = B * S
    depth = w_in.shape[0]
    assert D == D_MODEL and S % TM_IN == 0 and S % TM_MERGE == 0 and mem.shape[1] == N_MEM
    nk = S // TQ
    head_mean = jnp.kron(jnp.eye(FOX_HEADS, dtype=F32),
                         jnp.full((FOX_HEAD_DIM, FOX_HEAD_DIM), 1.0 / FOX_HEAD_DIM, F32)).astype(BF)

    x2d = x.reshape(T, D)
    for layer in range(depth):
        w = w_in[layer]
        wa = jnp.concatenate([w[:, OFF_POOL:OFF_F], w[:, OFF_MQ:OFF_G]], axis=1).astype(BF)
        wf = jnp.pad(w[:, OFF_F:OFF_MQ], ((0, 0), (0, LANES - FOX_HEADS))).astype(BF)
        wg = w[:, OFF_G:].astype(BF)
        qg = _row(jnp.tile(fox_q_g[layer], FOX_HEADS) * (FOX_HEAD_DIM ** -0.5 * LOG2E))
        kg = _row(jnp.tile(fox_k_g[layer], FOX_HEADS))
        mqg = _row(mem_q_g[layer] * MEM_HEAD_DIM ** -0.5)

        mkT, mv = _mem_kv(mem, _row(mem_norm_g[layer]), w_mem_kv[layer].astype(BF), _row(mem_k_g[layer]))
        u, q, kT, v, logf, mem_o = _in_proj(
            x2d, B, S, _row(mix_norm_g[layer]), wa, wf, _row(b_forget[layer], LANES), qg, kg,
            head_mean, mqg, mkT, mv)
        score_bound = (NORM_SLACK * FOX_HEAD_DIM * jnp.max(jnp.abs(qg)) * jnp.max(jnp.abs(kg)))
        fox_o = _fox(score_bound, q, _cumsum(logf, kT), v).reshape(T, FOX_WIDTH)

        merge_args = (x2d, S, u, fox_o, mem_o, _row(mix_norm_g[layer]), wg, pool_w[layer].astype(BF),
                      _row(pool_scale[layer]), w_pool_br[layer].astype(BF), w_fox_br[layer].astype(BF),
                      w_mem_br[layer].astype(BF), w_out[layer].astype(BF), _row(ffn_norm_g[layer]))
        if layer % 2 == 0:
            x2d = _merge_ffn(*merge_args, w_ffn_gu[layer // 2].astype(BF), w_ffn_down[layer // 2].astype(BF))
        else:
            m = layer // 2
            wr = jnp.pad(w_router[m], ((0, 0), (0, LANES - N_EXPERTS)))
            x1, h2, route = _merge_router(*merge_args, wr, _row(b_router[m], LANES))
            pos, pad_start, tile_expert, n_used, n_slots = _dispatch_plan(route, T)
            xs = _dispatch(h2, pos, pad_start, n_slots)
            y = _experts(xs, tile_expert, n_used, w_exp_gu[m].astype(BF), w_exp_down[m].astype(BF))
            x2d = _combine(x1, route, y, pos)
    return x2d.reshape(B, S, D)
```

```python
import functools

import jax
import jax.numpy as jnp
from jax import lax
from jax.experimental import pallas as pl
from jax.experimental.pallas import tpu as pltpu

D_MODEL = 1024
N_MEM = 256
POOL_WINDOWS = (2, 4, 8, 16)
POOL_WIDTH = 512
POOL_GROUP = 128
POOL_HALO = 16
FOX_HEADS = 8
FOX_HEAD_DIM = 64
FOX_WIDTH = 512
MEM_HEADS = 4
MEM_HEAD_DIM = 128
MEM_WIDTH = 512
N_BRANCH = 3
D_FF = 2816
N_EXPERTS = 8
TOP_K = 2
D_EXPERT = 3584
EPS = 1e-6

OFF_POOL = 0
OFF_Q = OFF_POOL + POOL_WIDTH
OFF_K = OFF_Q + FOX_WIDTH
OFF_V = OFF_K + FOX_WIDTH
OFF_F = OFF_V + FOX_WIDTH
OFF_MQ = OFF_F + FOX_HEADS
OFF_G = OFF_MQ + MEM_WIDTH

LANES = 128
LOG2E = 1.4426950408889634
BIAS_TERMS = 3
BIAS_ROWS = 16
SAFE_EXP2_SPAN = 100.0
NORM_SLACK = 1.02
BF = jnp.bfloat16
F32 = jnp.float32

TM_IN = 512
TQ = 512
TM_MERGE = 512
TM_EXPERT = 256
DISPATCH_ROWS = 512
FF_CHUNK = 256
VMEM_LIMIT = 58 * 1024 * 1024


def _resident(shape):
    nd = len(shape)
    return pl.BlockSpec(shape, lambda *_: (0,) * nd, pipeline_mode=pl.Buffered(1))


def _params(*sem):
    return pltpu.CompilerParams(dimension_semantics=sem, vmem_limit_bytes=VMEM_LIMIT)


def _dot(a, b):
    return jnp.dot(a, b, preferred_element_type=F32)


def _rms(x, g):
    ms = jnp.mean(x * x, axis=-1, keepdims=True)
    return x * lax.rsqrt(ms + EPS) * g


def _sigmoid(z):
    return 1.0 / (1.0 + jnp.exp(-z))


def _mem_kv_kernel(mem_ref, g_ref, w_ref, kg_ref, kT_ref, v_ref):
    mn = _rms(mem_ref[...], g_ref[...]).astype(BF)
    kv = _dot(mn, w_ref[...])
    for h in range(MEM_HEADS):
        kh = kv[:, h * MEM_HEAD_DIM:(h + 1) * MEM_HEAD_DIM]
        kT_ref[h] = _rms(kh, kg_ref[...]).T.astype(BF)
    v_ref[...] = kv[:, MEM_WIDTH:].astype(BF)


def _mem_kv(mem, g, w_kv, k_g):
    B, M, D = mem.shape
    return pl.pallas_call(
        _mem_kv_kernel,
        grid=(B,),
        in_specs=[pl.BlockSpec((None, M, D), lambda b: (b, 0, 0)),
                  _resident((1, D)), _resident((D, 2 * MEM_WIDTH)), _resident((1, MEM_HEAD_DIM))],
        out_specs=[pl.BlockSpec((None, MEM_HEADS, MEM_HEAD_DIM, M), lambda b: (b, 0, 0, 0)),
                   pl.BlockSpec((None, M, MEM_WIDTH), lambda b: (b, 0, 0))],
        out_shape=[jax.ShapeDtypeStruct((B, MEM_HEADS, MEM_HEAD_DIM, M), BF),
                   jax.ShapeDtypeStruct((B, M, MEM_WIDTH), BF)],
        compiler_params=_params("arbitrary"),
        name="mem_kv",
    )(mem, g, w_kv, k_g)


def _in_proj_kernel(x_ref, g_ref, wa_ref, wf_ref, bf_ref, qg_ref, kg_ref, bd_ref, mqg_ref,
                    mkT_ref, mv_ref, u_ref, q_ref, kT_ref, v_ref, lf_ref, mo_ref):
    tm = x_ref.shape[0]
    hb = _rms(x_ref[...], g_ref[...]).astype(BF)

    u_ref[...] = _dot(hb, wa_ref[:, OFF_POOL:OFF_Q]).astype(BF)

    lane = lax.broadcasted_iota(jnp.int32, (tm, LANES), 1)
    live = lane < FOX_HEAD_DIM

    def split_heads(z, n_ones, ref):
        pad = jnp.where(lane < FOX_HEAD_DIM + n_ones, 1.0, 0.0)
        for p in range(FOX_HEADS // 2):
            pair = z[:, p * LANES:(p + 1) * LANES]
            ref[2 * p] = jnp.where(live, pair, pad).astype(BF)
            ref[2 * p + 1] = jnp.where(live, pltpu.roll(pair, FOX_HEAD_DIM, axis=1), pad).astype(BF)

    bd = bd_ref[...]
    zq = _dot(hb, wa_ref[:, OFF_Q:OFF_K])
    qms = _dot((zq * zq).astype(BF), bd)
    split_heads(zq * lax.rsqrt(qms + EPS) * qg_ref[...], BIAS_TERMS, q_ref)
    zk = _dot(hb, wa_ref[:, OFF_K:OFF_V])
    kms = _dot((zk * zk).astype(BF), bd)
    kn = zk * lax.rsqrt(kms + EPS) * kg_ref[...]
    kT_ref[:, 0:FOX_HEAD_DIM, :] = kn.T.reshape(FOX_HEADS, FOX_HEAD_DIM, tm).astype(BF)
    kT_ref[:, FOX_HEAD_DIM:, :] = jnp.zeros((FOX_HEADS, LANES - FOX_HEAD_DIM, tm), BF)
    split_heads(_dot(hb, wa_ref[:, OFF_V:OFF_F]), 1, v_ref)

    zf = _dot(hb, wf_ref[...]) + bf_ref[...]
    logf = jnp.minimum(zf, 0.0) - jnp.log(1.0 + jnp.exp(-jnp.abs(zf)))
    lf_ref[...] = logf.T[0:FOX_HEADS, :]

    zm = _dot(hb, wa_ref[:, OFF_F:OFF_F + MEM_WIDTH])
    for h in range(MEM_HEADS):
        sl = slice(h * MEM_HEAD_DIM, (h + 1) * MEM_HEAD_DIM)
        qn = _rms(zm[:, sl], mqg_ref[...]).astype(BF)
        s = _dot(qn, mkT_ref[h])
        p = jnp.exp(s - jnp.max(s, axis=-1, keepdims=True))
        l = jnp.sum(p, axis=-1, keepdims=True)
        mo_ref[:, sl] = (_dot(p.astype(BF), mv_ref[:, sl]) / l).astype(BF)


def _in_proj(x2d, B, S, g, wa, wf, bfg, qg, kg, bd, mqg, mkT, mv):
    T, D = x2d.shape
    tm = TM_IN
    nk = S // tm
    wa_cols = wa.shape[1]
    M = mkT.shape[-1]
    tok = lambda i: (i, 0)
    return pl.pallas_call(
        _in_proj_kernel,
        grid=(T // tm,),
        in_specs=[pl.BlockSpec((tm, D), tok),
                  _resident((1, D)), _resident((D, wa_cols)), _resident((D, LANES)),
                  _resident((1, LANES)), _resident((1, FOX_WIDTH)), _resident((1, FOX_WIDTH)),
                  _resident((FOX_WIDTH, FOX_WIDTH)), _resident((1, MEM_HEAD_DIM)),
                  pl.BlockSpec((None, MEM_HEADS, MEM_HEAD_DIM, M), lambda i: (i // nk, 0, 0, 0)),
                  pl.BlockSpec((None, M, MEM_WIDTH), lambda i: (i // nk, 0, 0))],
        out_specs=[pl.BlockSpec((tm, POOL_WIDTH), tok),
                   pl.BlockSpec((None, FOX_HEADS, tm, LANES), lambda i: (i // nk, 0, i % nk, 0)),
                   pl.BlockSpec((None, FOX_HEADS, None, LANES, tm), lambda i: (i // nk, 0, i % nk, 0, 0)),
                   pl.BlockSpec((None, FOX_HEADS, tm, LANES), lambda i: (i // nk, 0, i % nk, 0)),
                   pl.BlockSpec((None, FOX_HEADS, tm), lambda i: (i // nk, 0, i % nk)),
                   pl.BlockSpec((tm, MEM_WIDTH), tok)],
        out_shape=[jax.ShapeDtypeStruct((T, POOL_WIDTH), BF),
                   jax.ShapeDtypeStruct((B, FOX_HEADS, S, LANES), BF),
                   jax.ShapeDtypeStruct((B, FOX_HEADS, nk, LANES, tm), BF),
                   jax.ShapeDtypeStruct((B, FOX_HEADS, S, LANES), BF),
                   jax.ShapeDtypeStruct((B, FOX_HEADS, S), F32),
                   jax.ShapeDtypeStruct((T, MEM_WIDTH), BF)],
        compiler_params=_params("arbitrary"),
        name="in_proj",
    )(x2d, g, wa, wf, bfg, qg, kg, bd, mqg, mkT, mv)


def _split_terms(x, n):
    terms = []
    for _ in range(n):
        t = x.astype(BF).astype(F32)
        terms.append(t)
        x = x - t
    return terms


def _cumsum_kernel(lf_ref, k_any_ref, krow_ref):
    del k_any_ref
    x = lf_ref[...]
    H, S = x.shape
    tk = krow_ref.shape[-1]
    lane = lax.broadcasted_iota(jnp.int32, x.shape, 1)
    k = 1
    while k < S:
        x = x + jnp.where(lane >= k, pltpu.roll(x, k, axis=1), 0.0)
        k *= 2
    terms = _split_terms(x * (-LOG2E), BIAS_TERMS)
    sub = lax.broadcasted_iota(jnp.int32, (BIAS_ROWS, tk), 0)
    ones_rows = jnp.where((sub >= BIAS_TERMS) & (sub < 2 * BIAS_TERMS), 1.0, 0.0)
    for j in range(S // tk):
        sl = slice(j * tk, (j + 1) * tk)
        for h in range(H):
            blk = ones_rows
            for r, t in enumerate(terms):
                blk = jnp.where(sub == r, t[h:h + 1, sl], blk)
            krow_ref[h, j] = blk.astype(BF)


def _cumsum(logf, kT):
    B, H, S = logf.shape
    nk, tk = kT.shape[2], kT.shape[4]
    assert FOX_HEAD_DIM % BIAS_ROWS == 0 and 2 * BIAS_TERMS <= BIAS_ROWS
    return pl.pallas_call(
        _cumsum_kernel, grid=(B,),
        in_specs=[pl.BlockSpec((None, H, S), lambda b: (b, 0, 0)),
                  pl.BlockSpec(memory_space=pl.ANY)],
        out_specs=pl.BlockSpec((None, H, nk, BIAS_ROWS, tk),
                               lambda b: (b, 0, 0, FOX_HEAD_DIM // BIAS_ROWS, 0)),
        out_shape=jax.ShapeDtypeStruct(kT.shape, kT.dtype),
        input_output_aliases={1: 0},
        compiler_params=_params("arbitrary"), name="forget_cumsum",
    )(logf, kT)


def _fox_kernel(exact_max, ub_ref, q_ref, kT_ref, v_ref, o_ref):
    tq = q_ref.shape[1]
    tk = kT_ref.shape[-1]
    i = pl.program_id(2)
    row = lax.broadcasted_iota(jnp.int32, (tq, tk), 0)
    col = lax.broadcasted_iota(jnp.int32, (tq, tk), 1)
    causal = col <= row
    lane = lax.broadcasted_iota(jnp.int32, (tq, LANES), 1)

    def scores(q, h, j, diagonal):
        z = _dot(q, kT_ref[h, j])
        return jnp.where(causal, z, -jnp.inf) if diagonal else z

    def max_step(j, carry, diagonal):
        out = []
        for h in range(2):
            z = scores(q_ref[h], h, j, diagonal)
            m = carry[h]
            for c in range(tk // LANES):
                m = jnp.maximum(m, z[:, c * LANES:(c + 1) * LANES])
            out.append(m)
        return tuple(out)

    if exact_max:
        m_lane = tuple(jnp.full((tq, LANES), -jnp.inf, F32) for _ in range(2))
        m_lane = max_step(i, lax.fori_loop(0, i, lambda j, c: max_step(j, c, False), m_lane), True)
    else:
        sub = lax.broadcasted_iota(jnp.int32, (LANES, LANES), 0)
        pick = jnp.where((sub >= FOX_HEAD_DIM) & (sub < FOX_HEAD_DIM + BIAS_TERMS), 1.0, 0.0).astype(BF)
        m_lane = tuple(lax.dot_general(kT_ref[h, i], pick, (((0,), (0,)), ((), ())),
                                       preferred_element_type=F32) + ub_ref[0] for h in range(2))
    q_stab = []
    for h in range(2):
        m_row = jnp.max(m_lane[h], axis=-1, keepdims=True)
        q = q_ref[h]
        for r, t in enumerate(_split_terms(-m_row, BIAS_TERMS)):
            q = jnp.where(lane == FOX_HEAD_DIM + BIAS_TERMS + r, t.astype(BF), q)
        q_stab.append(q)

    def pv_step(j, carry, diagonal):
        rows = pl.ds(pl.multiple_of(j * tk, tk), tk)
        out = []
        for h in range(2):
            p = jnp.exp2(scores(q_stab[h], h, j, diagonal))
            out.append(carry[h] + _dot(p.astype(BF), v_ref[h, rows, :]))
        return tuple(out)

    def pv_diagonal(carry):
        half = tq // 2
        kv_rows = pl.ds(pl.multiple_of(i * tk, tk), tk)
        out = []
        for h in range(2):
            kT = kT_ref[h, i]
            v = v_ref[h, kv_rows, :]
            z_top = jnp.where(causal[:half, :half], _dot(q_stab[h][:half], kT[:, :half]), -jnp.inf)
            z_bot = jnp.where(causal[half:, :], _dot(q_stab[h][half:], kT), -jnp.inf)
            a_top = _dot(jnp.exp2(z_top).astype(BF), v[:half])
            a_bot = _dot(jnp.exp2(z_bot).astype(BF), v)
            out.append(carry[h] + jnp.concatenate([a_top, a_bot], axis=0))
        return tuple(out)

    a_init = tuple(jnp.zeros((tq, LANES), F32) for _ in range(2))
    a0, a1 = pv_diagonal(lax.fori_loop(0, i, lambda j, c: pv_step(j, c, False), a_init))
    o0 = a0 / a0[:, FOX_HEAD_DIM:FOX_HEAD_DIM + 1]
    o1 = a1 / a1[:, FOX_HEAD_DIM:FOX_HEAD_DIM + 1]
    o_ref[...] = jnp.where(lane < FOX_HEAD_DIM, o0, pltpu.roll(o1, FOX_HEAD_DIM, axis=1)).astype(BF)


def _fox_call(exact_max, ub, q, kT, v):
    B, H, S, _ = q.shape
    nk = S // TQ
    grid_spec = pltpu.PrefetchScalarGridSpec(
        num_scalar_prefetch=1,
        grid=(B, H // 2, nk),
        in_specs=[pl.BlockSpec((None, 2, TQ, LANES), lambda b, h, i, ub: (b, h, i, 0)),
                  pl.BlockSpec((None, 2, nk, LANES, TQ), lambda b, h, i, ub: (b, h, 0, 0, 0)),
                  pl.BlockSpec((None, 2, S, LANES), lambda b, h, i, ub: (b, h, 0, 0))],
        out_specs=pl.BlockSpec((None, TQ, LANES), lambda b, h, i, ub: (b, i, h)),
    )
    return pl.pallas_call(
        functools.partial(_fox_kernel, exact_max),
        grid_spec=grid_spec,
        out_shape=jax.ShapeDtypeStruct((B, S, H * FOX_HEAD_DIM), BF),
        compiler_params=_params("arbitrary", "arbitrary", "arbitrary"),
        name="fox_attention_exact_max" if exact_max else "fox_attention",
    )(ub, q, kT, v)


def _fox(score_bound, q, kT, v):
    ub = score_bound.reshape(1).astype(F32)
    return lax.cond(2.0 * score_bound <= SAFE_EXP2_SPAN,
                    functools.partial(_fox_call, False), functools.partial(_fox_call, True),
                    ub, q, kT, v)


def _merge_body(n_seq_tiles, x_ref, u_ref, up_ref, fox_ref, mo_ref, g_ref, wg_ref, pw_ref, ps_ref,
                wp_ref, wfb_ref, wmb_ref, wo_ref):
    tm = x_ref.shape[0]
    i = pl.program_id(0)
    x = x_ref[...]
    hb = _rms(x, g_ref[...]).astype(BF)

    seq_tile = i % n_seq_tiles
    halo = jnp.where(seq_tile == 0, 0.0, up_ref[...].astype(F32))
    u = u_ref[...].astype(F32)
    ext = jnp.concatenate([halo, u], axis=0)
    pos = seq_tile * tm + lax.broadcasted_iota(jnp.int32, (tm, 1), 0)
    outs = []
    for g, w in enumerate(POOL_WINDOWS):
        sl = slice(g * POOL_GROUP, (g + 1) * POOL_GROUP)
        s = ext[:, sl]
        k = 1
        while k < w:
            s = s + pltpu.roll(s, k, axis=0)
            k *= 2
        cnt = jnp.minimum(pos + 1, w).astype(F32)
        d = s[POOL_HALO:, :] / cnt - u[:, sl]
        outs.append(_dot(d.astype(BF), pw_ref[g]))
    pool_o = (jnp.concatenate(outs, axis=1) * ps_ref[...]).astype(BF)

    merged = _sigmoid(_dot(hb, wg_ref[:, 0:D_MODEL])) * _dot(pool_o, wp_ref[...])
    merged += _sigmoid(_dot(hb, wg_ref[:, D_MODEL:2 * D_MODEL])) * _dot(fox_ref[...], wfb_ref[...])
    merged += _sigmoid(_dot(hb, wg_ref[:, 2 * D_MODEL:])) * _dot(mo_ref[...], wmb_ref[...])
    return x + _dot(merged.astype(BF), wo_ref[...])


def _swiglu(hb, wgu_ref, wd_ref, act_ref, d_hidden):
    for c in range(d_hidden // FF_CHUNK):
        lo = c * FF_CHUNK
        g = _dot(hb, wgu_ref[:, lo:lo + FF_CHUNK])
        up = _dot(hb, wgu_ref[:, d_hidden + lo:d_hidden + lo + FF_CHUNK])
        act_ref[:, lo:lo + FF_CHUNK] = (g * _sigmoid(g) * up).astype(BF)
    return _dot(act_ref[...], wd_ref[...])


def _merge_ffn_kernel(n_seq_tiles, x_ref, u_ref, up_ref, fox_ref, mo_ref, g_ref, wg_ref, pw_ref,
                      ps_ref, wp_ref, wfb_ref, wmb_ref, wo_ref, g2_ref, wgu_ref, wd_ref,
                      o_ref, act_ref):
    x1 = _merge_body(n_seq_tiles, x_ref, u_ref, up_ref, fox_ref, mo_ref, g_ref, wg_ref, pw_ref,
                     ps_ref, wp_ref, wfb_ref, wmb_ref, wo_ref)
    h2 = _rms(x1, g2_ref[...]).astype(BF)
    o_ref[...] = x1 + _swiglu(h2, wgu_ref, wd_ref, act_ref, D_FF)


def _split_bf16(a):
    hi = a.astype(BF)
    return hi, (a - hi.astype(F32)).astype(BF)


ROW_TILE = D_MODEL // LANES


def _store_row_tiles(ref, x, row0=0, stride=ROW_TILE):
    n = x.shape[0]
    for c in range(ROW_TILE):
        ref[pl.ds(row0 + c, n, stride=stride), :] = x[:, c * LANES:(c + 1) * LANES]


def _load_row_tiles(ref, n, row0=0, stride=ROW_TILE):
    return jnp.concatenate([ref[pl.ds(row0 + c, n, stride=stride), :] for c in range(ROW_TILE)], axis=1)


def _merge_router_kernel(n_seq_tiles, x_ref, u_ref, up_ref, fox_ref, mo_ref, g_ref, wg_ref, pw_ref,
                         ps_ref, wp_ref, wfb_ref, wmb_ref, wo_ref, g2_ref, wr_ref, br_ref,
                         x1_ref, h2_ref, route_ref):
    x1 = _merge_body(n_seq_tiles, x_ref, u_ref, up_ref, fox_ref, mo_ref, g_ref, wg_ref, pw_ref,
                     ps_ref, wp_ref, wfb_ref, wmb_ref, wo_ref)
    x1_ref[...] = x1
    h2 = _rms(x1, g2_ref[...])
    _store_row_tiles(h2_ref, h2)

    h_hi, h_lo = _split_bf16(h2)
    w_hi, w_lo = _split_bf16(wr_ref[...])
    logits = _dot(h_hi, w_hi) + (_dot(h_hi, w_lo) + _dot(h_lo, w_hi)) + br_ref[...]
    lane = lax.broadcasted_iota(jnp.int32, logits.shape, 1)
    lane_f = lane.astype(F32)
    logits = jnp.where(lane < N_EXPERTS, logits, -jnp.inf)
    m1 = jnp.max(logits, axis=-1, keepdims=True)
    i1 = jnp.min(jnp.where(logits == m1, lane_f, float(LANES)), axis=-1, keepdims=True)
    rest = jnp.where(lane_f == i1, -jnp.inf, logits)
    m2 = jnp.max(rest, axis=-1, keepdims=True)
    i2 = jnp.min(jnp.where(rest == m2, lane_f, float(LANES)), axis=-1, keepdims=True)
    e = jnp.exp(m2 - m1)
    w1 = 1.0 / (1.0 + e)
    w2 = e / (1.0 + e)
    route = jnp.where(lane == 0, i1, 0.0)
    route = jnp.where(lane == 1, i2, route)
    route = jnp.where(lane == 2, w1, route)
    route_ref[...] = jnp.where(lane == 3, w2, route)


def _merge_specs(T, S, tm):
    D = D_MODEL
    tok = lambda i: (i, 0)
    halo_blocks = tm // POOL_HALO
    in_specs = [pl.BlockSpec((tm, D), tok),
                pl.BlockSpec((tm, POOL_WIDTH), tok),
                pl.BlockSpec((POOL_HALO, POOL_WIDTH), lambda i: (jnp.maximum(i * halo_blocks - 1, 0), 0)),
                pl.BlockSpec((tm, FOX_WIDTH), tok),
                pl.BlockSpec((tm, MEM_WIDTH), tok),
                _resident((1, D)), _resident((D, N_BRANCH * D)),
                _resident((len(POOL_WINDOWS), POOL_GROUP, POOL_GROUP)), _resident((1, POOL_WIDTH)),
                _resident((POOL_WIDTH, D)), _resident((FOX_WIDTH, D)), _resident((MEM_WIDTH, D)),
                _resident((D, D)), _resident((1, D))]
    return in_specs, tok


def _merge_ffn(x2d, S, u, fox_o, mem_o, g, wg, pw, ps, wp, wfb, wmb, wo, g2, wgu, wd):
    T, D = x2d.shape
    tm = TM_MERGE
    in_specs, tok = _merge_specs(T, S, tm)
    in_specs += [_resident((D, 2 * D_FF)), _resident((D_FF, D))]
    return pl.pallas_call(
        functools.partial(_merge_ffn_kernel, S // tm),
        grid=(T // tm,),
        in_specs=in_specs,
        out_specs=pl.BlockSpec((tm, D), tok),
        out_shape=jax.ShapeDtypeStruct((T, D), F32),
        scratch_shapes=[pltpu.VMEM((tm, D_FF), BF)],
        compiler_params=_params("arbitrary"),
        name="merge_ffn",
    )(x2d, u, u, fox_o, mem_o, g, wg, pw, ps, wp, wfb, wmb, wo, g2, wgu, wd)


def _merge_router(x2d, S, u, fox_o, mem_o, g, wg, pw, ps, wp, wfb, wmb, wo, g2, wr, br):
    T, D = x2d.shape
    tm = TM_MERGE
    in_specs, tok = _merge_specs(T, S, tm)
    in_specs += [_resident((D, LANES)), _resident((1, LANES))]
    return pl.pallas_call(
        functools.partial(_merge_router_kernel, S // tm),
        grid=(T // tm,),
        in_specs=in_specs,
        out_specs=[pl.BlockSpec((tm, D), tok), pl.BlockSpec((tm * ROW_TILE, LANES), tok),
                   pl.BlockSpec((tm, LANES), tok)],
        out_shape=[jax.ShapeDtypeStruct((T, D), F32), jax.ShapeDtypeStruct((T * ROW_TILE, LANES), F32),
                   jax.ShapeDtypeStruct((T, LANES), F32)],
        compiler_params=_params("arbitrary"),
        name="merge_router",
    )(x2d, u, u, fox_o, mem_o, g, wg, pw, ps, wp, wfb, wmb, wo, g2, wr, br)


def _row_tile(ref, r):
    start = r * ROW_TILE if isinstance(r, int) else pl.multiple_of(r * ROW_TILE, ROW_TILE)
    return ref.at[pl.ds(start, ROW_TILE)]


def _dispatch_kernel(pad_ref, pos_ref, zeros_ref, h_ref, xs_ref, sem):
    i = pl.program_id(0)
    n = pos_ref.shape[-1]
    pad_rows = zeros_ref.shape[0]

    def zero_fill(first_slot):
        dst = xs_ref.at[pl.ds(pl.multiple_of(first_slot * ROW_TILE, ROW_TILE), pad_rows)]
        fill = pltpu.make_async_copy(zeros_ref, dst, sem)
        fill.start()
        fill.wait()

    @pl.when(i == 0)
    def _():
        for e in range(N_EXPERTS):
            zero_fill(pad_ref[e])
        tile_slots = pad_rows // ROW_TILE

        def tail(t, carry):
            zero_fill(t * tile_slots)
            return carry

        lax.fori_loop(pad_ref[N_EXPERTS], xs_ref.shape[0] // pad_rows, tail, 0)

    def row_copy(t, slot):
        return pltpu.make_async_copy(_row_tile(h_ref, t), _row_tile(xs_ref, slot), sem)

    def issue(t, carry):
        for k in range(TOP_K):
            row_copy(t, pos_ref[0, 0, TOP_K * t + k]).start(priority=k % 2)
        return carry

    def drain(r, carry):
        row_copy(0, 0).wait()
        return carry

    lax.fori_loop(0, n // TOP_K, issue, 0, unroll=4)
    lax.fori_loop(0, n, drain, 0, unroll=8)


def _dispatch(h_tiles, pos, pad_start, n_slots):
    n = pos.shape[0]
    R = DISPATCH_ROWS
    grid_spec = pltpu.PrefetchScalarGridSpec(
        num_scalar_prefetch=1,
        grid=(n // R,),
        in_specs=[pl.BlockSpec((1, 1, R), lambda i, pad: (i, 0, 0), memory_space=pltpu.SMEM),
                  pl.BlockSpec(memory_space=pl.ANY),
                  pl.BlockSpec((R // TOP_K * ROW_TILE, LANES), lambda i, pad: (i, 0))],
        out_specs=pl.BlockSpec(memory_space=pl.ANY),
        scratch_shapes=[pltpu.SemaphoreType.DMA],
    )
    zeros = jnp.zeros((TM_EXPERT * ROW_TILE, LANES), F32)
    return pl.pallas_call(
        _dispatch_kernel,
        grid_spec=grid_spec,
        out_shape=jax.ShapeDtypeStruct((n_slots * ROW_TILE, LANES), F32),
        compiler_params=_params("arbitrary"),
        name="moe_dispatch",
    )(pad_start, pos.reshape(n // R, 1, R), zeros, h_tiles)


def _expert_kernel(te_ref, nu_ref, xs_ref, wgu_ref, wd_ref, y_ref, act_ref):
    i = pl.program_id(0)
    tm = act_ref.shape[0]

    @pl.when(i < nu_ref[0])
    def _():
        xb = _load_row_tiles(xs_ref, tm).astype(BF)
        _store_row_tiles(y_ref, _swiglu(xb, wgu_ref, wd_ref, act_ref, D_EXPERT))

    @pl.when(i >= nu_ref[0])
    def _():
        y_ref[...] = jnp.zeros_like(y_ref)


def _experts(xs, tile_expert, n_used, wgu, wd):
    D = D_MODEL
    tm = TM_EXPERT
    blk = tm * ROW_TILE
    used = lambda i, te, nu: (jnp.minimum(i, nu[0] - 1), 0)
    grid_spec = pltpu.PrefetchScalarGridSpec(
        num_scalar_prefetch=2,
        grid=(xs.shape[0] // blk,),
        in_specs=[pl.BlockSpec((blk, LANES), used),
                  pl.BlockSpec((None, D, 2 * D_EXPERT), lambda i, te, nu: (te[i], 0, 0)),
                  pl.BlockSpec((None, D_EXPERT, D), lambda i, te, nu: (te[i], 0, 0))],
        out_specs=pl.BlockSpec((blk, LANES), lambda i, te, nu: (i, 0)),
        scratch_shapes=[pltpu.VMEM((tm, D_EXPERT), BF)],
    )
    return pl.pallas_call(
        _expert_kernel,
        grid_spec=grid_spec,
        out_shape=jax.ShapeDtypeStruct(xs.shape, F32),
        compiler_params=_params("arbitrary"),
        name="experts",
    )(tile_expert, n_used, xs, wgu, wd)


def _combine_kernel(pos_ref, x1_ref, route_ref, y_ref, o_ref, yv_ref, sem):
    tm = x1_ref.shape[0]
    n = pos_ref.shape[-1]

    def row_copy(slot, r):
        return pltpu.make_async_copy(_row_tile(y_ref, slot), _row_tile(yv_ref, r), sem)

    def issue(t, carry):
        for k in range(TOP_K):
            r = TOP_K * t + k
            row_copy(pos_ref[0, 0, r], r).start(priority=k % 2)
        return carry

    def drain(r, carry):
        row_copy(0, 0).wait()
        return carry

    lax.fori_loop(0, n // TOP_K, issue, 0, unroll=4)
    lax.fori_loop(0, n, drain, 0, unroll=8)
    w = route_ref[...]
    y0 = _load_row_tiles(yv_ref, tm, 0, TOP_K * ROW_TILE)
    y1 = _load_row_tiles(yv_ref, tm, ROW_TILE, TOP_K * ROW_TILE)
    o_ref[...] = x1_ref[...] + w[:, 2:3] * y0 + w[:, 3:4] * y1


def _combine(x1, route, y_tiles, pos):
    T, D = x1.shape
    tm = TM_MERGE
    n = TOP_K * tm
    return pl.pallas_call(
        _combine_kernel,
        grid=(T // tm,),
        in_specs=[pl.BlockSpec((1, 1, n), lambda i: (i, 0, 0), memory_space=pltpu.SMEM),
                  pl.BlockSpec((tm, D), lambda i: (i, 0)),
                  pl.BlockSpec((tm, LANES), lambda i: (i, 0)),
                  pl.BlockSpec(memory_space=pl.ANY)],
        out_specs=pl.BlockSpec((tm, D), lambda i: (i, 0)),
        out_shape=jax.ShapeDtypeStruct((T, D), F32),
        scratch_shapes=[pltpu.VMEM((n * ROW_TILE, LANES), F32), pltpu.SemaphoreType.DMA],
        compiler_params=_params("arbitrary"),
        name="moe_combine",
    )(pos.reshape(T // tm, 1, n), x1, route, y_tiles)


def _dispatch_plan(route, T):
    tm = TM_EXPERT
    n_slots = TOP_K * T + (N_EXPERTS + 1) * tm
    n_tiles = n_slots // tm
    e = route[:, :TOP_K].astype(jnp.int32).reshape(-1)
    onehot = (e[:, None] == jnp.arange(N_EXPERTS, dtype=jnp.int32)[None, :]).astype(jnp.int32)
    csum = jnp.cumsum(onehot, axis=0)
    counts = csum[-1]
    rank = jnp.take_along_axis(csum, e[:, None], axis=1)[:, 0] - 1
    padded = ((counts + tm - 1) // tm) * tm
    ends = jnp.cumsum(padded)
    starts = ends - padded
    pos = (starts[e] + rank).astype(jnp.int32)
    tile_ends = ends // tm
    pad_start = jnp.concatenate([starts + counts, tile_ends[-1:]]).astype(jnp.int32)
    tile_ids = jnp.arange(n_tiles, dtype=jnp.int32)
    tile_expert = jnp.minimum(jnp.sum((tile_ids[:, None] >= tile_ends[None, :]).astype(jnp.int32), axis=1),
                              N_EXPERTS - 1)
    n_used = tile_ends[-1:].astype(jnp.int32)
    return pos, pad_start, tile_expert, n_used, n_slots


def _row(v, width=None):
    v = v.astype(F32).reshape(1, -1)
    if width is not None and v.shape[1] < width:
        v = jnp.pad(v, ((0, 0), (0, width - v.shape[1])))
    return v


def kernel(x, mem, mix_norm_g, w_in, b_forget, fox_q_g, fox_k_g, pool_w, pool_scale, mem_norm_g,
           w_mem_kv, mem_q_g, mem_k_g, w_pool_br, w_fox_br, w_mem_br, w_out, ffn_norm_g, w_ffn_gu,
           w_ffn_down, w_router, b_router, w_exp_gu, w_exp_down):
    B, S, D = x.shape
    T = B * S
    depth = w_in.shape[0]
    assert D == D_MODEL and S % TM_IN == 0 and S % TM_MERGE == 0 and mem.shape[1] == N_MEM
    nk = S // TQ
    head_mean = jnp.kron(jnp.eye(FOX_HEADS, dtype=F32),
                         jnp.full((FOX_HEAD_DIM, FOX_HEAD_DIM), 1.0 / FOX_HEAD_DIM, F32)).astype(BF)

    x2d = x.reshape(T, D)
    for layer in range(depth):
        w = w_in[layer]
        wa = jnp.concatenate([w[:, OFF_POOL:OFF_F], w[:, OFF_MQ:OFF_G]], axis=1).astype(BF)
        wf = jnp.pad(w[:, OFF_F:OFF_MQ], ((0, 0), (0, LANES - FOX_HEADS))).astype(BF)
        wg = w[:, OFF_G:].astype(BF)
        qg = _row(jnp.tile(fox_q_g[layer], FOX_HEADS) * (FOX_HEAD_DIM ** -0.5 * LOG2E))
        kg = _row(jnp.tile(fox_k_g[layer], FOX_HEADS))
        mqg = _row(mem_q_g[layer] * MEM_HEAD_DIM ** -0.5)

        mkT, mv = _mem_kv(mem, _row(mem_norm_g[layer]), w_mem_kv[layer].astype(BF), _row(mem_k_g[layer]))
        u, q, kT, v, logf, mem_o = _in_proj(
            x2d, B, S, _row(mix_norm_g[layer]), wa, wf, _row(b_forget[layer], LANES), qg, kg,
            head_mean, mqg, mkT, mv)
        score_bound = (NORM_SLACK * FOX_HEAD_DIM * jnp.max(jnp.abs(qg)) * jnp.max(jnp.abs(kg)))
        fox_o = _fox(score_bound, q, _cumsum(logf, kT), v).reshape(T, FOX_WIDTH)

        merge_args = (x2d, S, u, fox_o, mem_o, _row(mix_norm_g[layer]), wg, pool_w[layer].astype(BF),
                      _row(pool_scale[layer]), w_pool_br[layer].astype(BF), w_fox_br[layer].astype(BF),
                      w_mem_br[layer].astype(BF), w_out[layer].astype(BF), _row(ffn_norm_g[layer]))
        if layer % 2 == 0:
            x2d = _merge_ffn(*merge_args, w_ffn_gu[layer // 2].astype(BF), w_ffn_down[layer // 2].astype(BF))
        else:
            m = layer // 2
            wr = jnp.pad(w_router[m], ((0, 0), (0, LANES - N_EXPERTS)))
            x1, h2, route = _merge_router(*merge_args, wr, _row(b_router[m], LANES))
            pos, pad_start, tile_expert, n_used, n_slots = _dispatch_plan(route, T)
            xs = _dispatch(h2, pos, pad_start, n_slots)
            y = _experts(xs, tile_expert, n_used, w_exp_gu[m].astype(BF), w_exp_down[m].astype(BF))
            x2d = _combine(x1, route, y, pos)
    return x2d.reshape(B, S, D)
```

```python
import functools

import jax
import jax.numpy as jnp
from jax import lax
from jax.experimental import pallas as pl
from jax.experimental.pallas import tpu as pltpu

D_MODEL = 1024
N_MEM = 256
POOL_WINDOWS = (2, 4, 8, 16)
POOL_WIDTH = 512
POOL_GROUP = 128
POOL_HALO = 16
FOX_HEADS = 8
FOX_HEAD_DIM = 64
FOX_WIDTH = 512
MEM_HEADS = 4
MEM_HEAD_DIM = 128
MEM_WIDTH = 512
N_BRANCH = 3
D_FF = 2816
N_EXPERTS = 8
TOP_K = 2
D_EXPERT = 3584
EPS = 1e-6

OFF_POOL = 0
OFF_Q = OFF_POOL + POOL_WIDTH
OFF_K = OFF_Q + FOX_WIDTH
OFF_V = OFF_K + FOX_WIDTH
OFF_F = OFF_V + FOX_WIDTH
OFF_MQ = OFF_F + FOX_HEADS
OFF_G = OFF_MQ + MEM_WIDTH

LANES = 128
LOG2E = 1.4426950408889634
BIAS_TERMS = 3
BIAS_ROWS = 16
SAFE_EXP2_SPAN = 100.0
NORM_SLACK = 1.02
BF = jnp.bfloat16
F32 = jnp.float32

TM_IN = 512
TQ = 512
TM_MERGE = 512
TM_EXPERT = 256
DISPATCH_ROWS = 1024
FF_CHUNK = 256
VMEM_LIMIT = 58 * 1024 * 1024


def _resident(shape):
    nd = len(shape)
    return pl.BlockSpec(shape, lambda *_: (0,) * nd, pipeline_mode=pl.Buffered(1))


def _params(*sem):
    return pltpu.CompilerParams(dimension_semantics=sem, vmem_limit_bytes=VMEM_LIMIT)


def _dot(a, b):
    return jnp.dot(a, b, preferred_element_type=F32)


def _rms(x, g):
    ms = jnp.mean(x * x, axis=-1, keepdims=True)
    return x * lax.rsqrt(ms + EPS) * g


def _sigmoid(z):
    return 1.0 / (1.0 + jnp.exp(-z))


def _mem_kv_kernel(mem_ref, g_ref, w_ref, kg_ref, kT_ref, v_ref):
    mn = _rms(mem_ref[...], g_ref[...]).astype(BF)
    kv = _dot(mn, w_ref[...])
    for h in range(MEM_HEADS):
        kh = kv[:, h * MEM_HEAD_DIM:(h + 1) * MEM_HEAD_DIM]
        kT_ref[h] = _rms(kh, kg_ref[...]).T.astype(BF)
    v_ref[...] = kv[:, MEM_WIDTH:].astype(BF)


def _mem_kv(mem, g, w_kv, k_g):
    B, M, D = mem.shape
    return pl.pallas_call(
        _mem_kv_kernel,
        grid=(B,),
        in_specs=[pl.BlockSpec((None, M, D), lambda b: (b, 0, 0)),
                  _resident((1, D)), _resident((D, 2 * MEM_WIDTH)), _resident((1, MEM_HEAD_DIM))],
        out_specs=[pl.BlockSpec((None, MEM_HEADS, MEM_HEAD_DIM, M), lambda b: (b, 0, 0, 0)),
                   pl.BlockSpec((None, M, MEM_WIDTH), lambda b: (b, 0, 0))],
        out_shape=[jax.ShapeDtypeStruct((B, MEM_HEADS, MEM_HEAD_DIM, M), BF),
                   jax.ShapeDtypeStruct((B, M, MEM_WIDTH), BF)],
        compiler_params=_params("arbitrary"),
        name="mem_kv",
    )(mem, g, w_kv, k_g)


def _in_proj_kernel(x_ref, g_ref, wa_ref, wf_ref, bf_ref, qg_ref, kg_ref, bd_ref, mqg_ref,
                    mkT_ref, mv_ref, u_ref, q_ref, kT_ref, v_ref, lf_ref, mo_ref):
    tm = x_ref.shape[0]
    hb = _rms(x_ref[...], g_ref[...]).astype(BF)

    u_ref[...] = _dot(hb, wa_ref[:, OFF_POOL:OFF_Q]).astype(BF)

    lane = lax.broadcasted_iota(jnp.int32, (tm, LANES), 1)
    live = lane < FOX_HEAD_DIM

    def split_heads(z, n_ones, ref):
        pad = jnp.where(lane < FOX_HEAD_DIM + n_ones, 1.0, 0.0)
        for p in range(FOX_HEADS // 2):
            pair = z[:, p * LANES:(p + 1) * LANES]
            ref[2 * p] = jnp.where(live, pair, pad).astype(BF)
            ref[2 * p + 1] = jnp.where(live, pltpu.roll(pair, FOX_HEAD_DIM, axis=1), pad).astype(BF)

    bd = bd_ref[...]
    zq = _dot(hb, wa_ref[:, OFF_Q:OFF_K])
    qms = _dot((zq * zq).astype(BF), bd)
    split_heads(zq * lax.rsqrt(qms + EPS) * qg_ref[...], BIAS_TERMS, q_ref)
    zk = _dot(hb, wa_ref[:, OFF_K:OFF_V])
    kms = _dot((zk * zk).astype(BF), bd)
    kn = zk * lax.rsqrt(kms + EPS) * kg_ref[...]
    kT_ref[:, 0:FOX_HEAD_DIM, :] = kn.T.reshape(FOX_HEADS, FOX_HEAD_DIM, tm).astype(BF)
    kT_ref[:, FOX_HEAD_DIM:, :] = jnp.zeros((FOX_HEADS, LANES - FOX_HEAD_DIM, tm), BF)
    split_heads(_dot(hb, wa_ref[:, OFF_V:OFF_F]), 1, v_ref)

    zf = _dot(hb, wf_ref[...]) + bf_ref[...]
    logf = jnp.minimum(zf, 0.0) - jnp.log(1.0 + jnp.exp(-jnp.abs(zf)))
    lf_ref[...] = logf.T[0:FOX_HEADS, :]

    zm = _dot(hb, wa_ref[:, OFF_F:OFF_F + MEM_WIDTH])
    for h in range(MEM_HEADS):
        sl = slice(h * MEM_HEAD_DIM, (h + 1) * MEM_HEAD_DIM)
        qn = _rms(zm[:, sl], mqg_ref[...]).astype(BF)
        s = _dot(qn, mkT_ref[h])
        p = jnp.exp(s - jnp.max(s, axis=-1, keepdims=True))
        l = jnp.sum(p, axis=-1, keepdims=True)
        mo_ref[:, sl] = (_dot(p.astype(BF), mv_ref[:, sl]) / l).astype(BF)


def _in_proj(x2d, B, S, g, wa, wf, bfg, qg, kg, bd, mqg, mkT, mv):
    T, D = x2d.shape
    tm = TM_IN
    nk = S // tm
    wa_cols = wa.shape[1]
    M = mkT.shape[-1]
    tok = lambda i: (i, 0)
    return pl.pallas_call(
        _in_proj_kernel,
        grid=(T // tm,),
        in_specs=[pl.BlockSpec((tm, D), tok),
                  _resident((1, D)), _resident((D, wa_cols)), _resident((D, LANES)),
                  _resident((1, LANES)), _resident((1, FOX_WIDTH)), _resident((1, FOX_WIDTH)),
                  _resident((FOX_WIDTH, FOX_WIDTH)), _resident((1, MEM_HEAD_DIM)),
                  pl.BlockSpec((None, MEM_HEADS, MEM_HEAD_DIM, M), lambda i: (i // nk, 0, 0, 0)),
                  pl.BlockSpec((None, M, MEM_WIDTH), lambda i: (i // nk, 0, 0))],
        out_specs=[pl.BlockSpec((tm, POOL_WIDTH), tok),
                   pl.BlockSpec((None, FOX_HEADS, tm, LANES), lambda i: (i // nk, 0, i % nk, 0)),
                   pl.BlockSpec((None, FOX_HEADS, None, LANES, tm), lambda i: (i // nk, 0, i % nk, 0, 0)),
                   pl.BlockSpec((None, FOX_HEADS, tm, LANES), lambda i: (i // nk, 0, i % nk, 0)),
                   pl.BlockSpec((None, FOX_HEADS, tm), lambda i: (i // nk, 0, i % nk)),
                   pl.BlockSpec((tm, MEM_WIDTH), tok)],
        out_shape=[jax.ShapeDtypeStruct((T, POOL_WIDTH), BF),
                   jax.ShapeDtypeStruct((B, FOX_HEADS, S, LANES), BF),
                   jax.ShapeDtypeStruct((B, FOX_HEADS, nk, LANES, tm), BF),
                   jax.ShapeDtypeStruct((B, FOX_HEADS, S, LANES), BF),
                   jax.ShapeDtypeStruct((B, FOX_HEADS, S), F32),
                   jax.ShapeDtypeStruct((T, MEM_WIDTH), BF)],
        compiler_params=_params("arbitrary"),
        name="in_proj",
    )(x2d, g, wa, wf, bfg, qg, kg, bd, mqg, mkT, mv)


def _split_terms(x, n):
    terms = []
    for _ in range(n):
        t = x.astype(BF).astype(F32)
        terms.append(t)
        x = x - t
    return terms


def _cumsum_kernel(lf_ref, k_any_ref, krow_ref):
    del k_any_ref
    x = lf_ref[...]
    H, S = x.shape
    tk = krow_ref.shape[-1]
    lane = lax.broadcasted_iota(jnp.int32, x.shape, 1)
    k = 1
    while k < S:
        x = x + jnp.where(lane >= k, pltpu.roll(x, k, axis=1), 0.0)
        k *= 2
    terms = _split_terms(x * (-LOG2E), BIAS_TERMS)
    sub = lax.broadcasted_iota(jnp.int32, (BIAS_ROWS, tk), 0)
    ones_rows = jnp.where((sub >= BIAS_TERMS) & (sub < 2 * BIAS_TERMS), 1.0, 0.0)
    for j in range(S // tk):
        sl = slice(j * tk, (j + 1) * tk)
        for h in range(H):
            blk = ones_rows
            for r, t in enumerate(terms):
                blk = jnp.where(sub == r, t[h:h + 1, sl], blk)
            krow_ref[h, j] = blk.astype(BF)


def _cumsum(logf, kT):
    B, H, S = logf.shape
    nk, tk = kT.shape[2], kT.shape[4]
    assert FOX_HEAD_DIM % BIAS_ROWS == 0 and 2 * BIAS_TERMS <= BIAS_ROWS
    return pl.pallas_call(
        _cumsum_kernel, grid=(B,),
        in_specs=[pl.BlockSpec((None, H, S), lambda b: (b, 0, 0)),
                  pl.BlockSpec(memory_space=pl.ANY)],
        out_specs=pl.BlockSpec((None, H, nk, BIAS_ROWS, tk),
                               lambda b: (b, 0, 0, FOX_HEAD_DIM // BIAS_ROWS, 0)),
        out_shape=jax.ShapeDtypeStruct(kT.shape, kT.dtype),
        input_output_aliases={1: 0},
        compiler_params=_params("arbitrary"), name="forget_cumsum",
    )(logf, kT)


def _fox_kernel(exact_max, ub_ref, q_ref, kT_ref, v_ref, o_ref):
    tq = q_ref.shape[1]
    tk = kT_ref.shape[-1]
    i = pl.program_id(2)
    row = lax.broadcasted_iota(jnp.int32, (tq, tk), 0)
    col = lax.broadcasted_iota(jnp.int32, (tq, tk), 1)
    causal = col <= row
    lane = lax.broadcasted_iota(jnp.int32, (tq, LANES), 1)

    def scores(q, h, j, diagonal):
        z = _dot(q, kT_ref[h, j])
        return jnp.where(causal, z, -jnp.inf) if diagonal else z

    def max_step(j, carry, diagonal):
        out = []
        for h in range(2):
            z = scores(q_ref[h], h, j, diagonal)
            m = carry[h]
            for c in range(tk // LANES):
                m = jnp.maximum(m, z[:, c * LANES:(c + 1) * LANES])
            out.append(m)
        return tuple(out)

    if exact_max:
        m_lane = tuple(jnp.full((tq, LANES), -jnp.inf, F32) for _ in range(2))
        m_lane = max_step(i, lax.fori_loop(0, i, lambda j, c: max_step(j, c, False), m_lane), True)
    else:
        sub = lax.broadcasted_iota(jnp.int32, (LANES, LANES), 0)
        pick = jnp.where((sub >= FOX_HEAD_DIM) & (sub < FOX_HEAD_DIM + BIAS_TERMS), 1.0, 0.0).astype(BF)
        m_lane = tuple(lax.dot_general(kT_ref[h, i], pick, (((0,), (0,)), ((), ())),
                                       preferred_element_type=F32) + ub_ref[0] for h in range(2))
    q_stab = []
    for h in range(2):
        m_row = jnp.max(m_lane[h], axis=-1, keepdims=True)
        q = q_ref[h]
        for r, t in enumerate(_split_terms(-m_row, BIAS_TERMS)):
            q = jnp.where(lane == FOX_HEAD_DIM + BIAS_TERMS + r, t.astype(BF), q)
        q_stab.append(q)

    def pv_step(j, carry, diagonal):
        rows = pl.ds(pl.multiple_of(j * tk, tk), tk)
        out = []
        for h in range(2):
            p = jnp.exp2(scores(q_stab[h], h, j, diagonal))
            out.append(carry[h] + _dot(p.astype(BF), v_ref[h, rows, :]))
        return tuple(out)

    def pv_diagonal(carry):
        half = tq // 2
        kv_rows = pl.ds(pl.multiple_of(i * tk, tk), tk)
        out = []
        for h in range(2):
            kT = kT_ref[h, i]
            v = v_ref[h, kv_rows, :]
            z_top = jnp.where(causal[:half, :half], _dot(q_stab[h][:half], kT[:, :half]), -jnp.inf)
            z_bot = jnp.where(causal[half:, :], _dot(q_stab[h][half:], kT), -jnp.inf)
            a_top = _dot(jnp.exp2(z_top).astype(BF), v[:half])
            a_bot = _dot(jnp.exp2(z_bot).astype(BF), v)
            out.append(carry[h] + jnp.concatenate([a_top, a_bot], axis=0))
        return tuple(out)

    a_init = tuple(jnp.zeros((tq, LANES), F32) for _ in range(2))
    a0, a1 = pv_diagonal(lax.fori_loop(0, i, lambda j, c: pv_step(j, c, False), a_init))
    o0 = a0 / a0[:, FOX_HEAD_DIM:FOX_HEAD_DIM + 1]
    o1 = a1 / a1[:, FOX_HEAD_DIM:FOX_HEAD_DIM + 1]
    o_ref[...] = jnp.where(lane < FOX_HEAD_DIM, o0, pltpu.roll(o1, FOX_HEAD_DIM, axis=1)).astype(BF)


def _fox_call(exact_max, ub, q, kT, v):
    B, H, S, _ = q.shape
    nk = S // TQ
    grid_spec = pltpu.PrefetchScalarGridSpec(
        num_scalar_prefetch=1,
        grid=(B, H // 2, nk),
        in_specs=[pl.BlockSpec((None, 2, TQ, LANES), lambda b, h, i, ub: (b, h, i, 0)),
                  pl.BlockSpec((None, 2, nk, LANES, TQ), lambda b, h, i, ub: (b, h, 0, 0, 0)),
                  pl.BlockSpec((None, 2, S, LANES), lambda b, h, i, ub: (b, h, 0, 0))],
        out_specs=pl.BlockSpec((None, TQ, LANES), lambda b, h, i, ub: (b, i, h)),
    )
    return pl.pallas_call(
        functools.partial(_fox_kernel, exact_max),
        grid_spec=grid_spec,
        out_shape=jax.ShapeDtypeStruct((B, S, H * FOX_HEAD_DIM), BF),
        compiler_params=_params("arbitrary", "arbitrary", "arbitrary"),
        name="fox_attention_exact_max" if exact_max else "fox_attention",
    )(ub, q, kT, v)


def _fox(score_bound, q, kT, v):
    ub = score_bound.reshape(1).astype(F32)
    return lax.cond(2.0 * score_bound <= SAFE_EXP2_SPAN,
                    functools.partial(_fox_call, False), functools.partial(_fox_call, True),
                    ub, q, kT, v)


def _merge_body(n_seq_tiles, x_ref, u_ref, up_ref, fox_ref, mo_ref, g_ref, wg_ref, pw_ref, ps_ref,
                wp_ref, wfb_ref, wmb_ref, wo_ref):
    tm = x_ref.shape[0]
    i = pl.program_id(0)
    x = x_ref[...]
    hb = _rms(x, g_ref[...]).astype(BF)

    seq_tile = i % n_seq_tiles
    halo = jnp.where(seq_tile == 0, 0.0, up_ref[...].astype(F32))
    u = u_ref[...].astype(F32)
    ext = jnp.concatenate([halo, u], axis=0)
    pos = seq_tile * tm + lax.broadcasted_iota(jnp.int32, (tm, 1), 0)
    outs = []
    for g, w in enumerate(POOL_WINDOWS):
        sl = slice(g * POOL_GROUP, (g + 1) * POOL_GROUP)
        s = ext[:, sl]
        k = 1
        while k < w:
            s = s + pltpu.roll(s, k, axis=0)
            k *= 2
        cnt = jnp.minimum(pos + 1, w).astype(F32)
        d = s[POOL_HALO:, :] / cnt - u[:, sl]
        outs.append(_dot(d.astype(BF), pw_ref[g]))
    pool_o = (jnp.concatenate(outs, axis=1) * ps_ref[...]).astype(BF)

    merged = _sigmoid(_dot(hb, wg_ref[:, 0:D_MODEL])) * _dot(pool_o, wp_ref[...])
    merged += _sigmoid(_dot(hb, wg_ref[:, D_MODEL:2 * D_MODEL])) * _dot(fox_ref[...], wfb_ref[...])
    merged += _sigmoid(_dot(hb, wg_ref[:, 2 * D_MODEL:])) * _dot(mo_ref[...], wmb_ref[...])
    return x + _dot(merged.astype(BF), wo_ref[...])


def _swiglu(hb, wgu_ref, wd_ref, act_ref, d_hidden):
    for c in range(d_hidden // FF_CHUNK):
        lo = c * FF_CHUNK
        g = _dot(hb, wgu_ref[:, lo:lo + FF_CHUNK])
        up = _dot(hb, wgu_ref[:, d_hidden + lo:d_hidden + lo + FF_CHUNK])
        act_ref[:, lo:lo + FF_CHUNK] = (g * _sigmoid(g) * up).astype(BF)
    return _dot(act_ref[...], wd_ref[...])


def _merge_ffn_kernel(n_seq_tiles, x_ref, u_ref, up_ref, fox_ref, mo_ref, g_ref, wg_ref, pw_ref,
                      ps_ref, wp_ref, wfb_ref, wmb_ref, wo_ref, g2_ref, wgu_ref, wd_ref,
                      o_ref, act_ref):
    x1 = _merge_body(n_seq_tiles, x_ref, u_ref, up_ref, fox_ref, mo_ref, g_ref, wg_ref, pw_ref,
                     ps_ref, wp_ref, wfb_ref, wmb_ref, wo_ref)
    h2 = _rms(x1, g2_ref[...]).astype(BF)
    o_ref[...] = x1 + _swiglu(h2, wgu_ref, wd_ref, act_ref, D_FF)


def _split_bf16(a):
    hi = a.astype(BF)
    return hi, (a - hi.astype(F32)).astype(BF)


ROW_TILE = D_MODEL // LANES


def _store_row_tiles(ref, x, row0=0, stride=ROW_TILE):
    n = x.shape[0]
    for c in range(ROW_TILE):
        ref[pl.ds(row0 + c, n, stride=stride), :] = x[:, c * LANES:(c + 1) * LANES]


def _load_row_tiles(ref, n, row0=0, stride=ROW_TILE):
    return jnp.concatenate([ref[pl.ds(row0 + c, n, stride=stride), :] for c in range(ROW_TILE)], axis=1)


def _merge_router_kernel(n_seq_tiles, x_ref, u_ref, up_ref, fox_ref, mo_ref, g_ref, wg_ref, pw_ref,
                         ps_ref, wp_ref, wfb_ref, wmb_ref, wo_ref, g2_ref, wr_ref, br_ref,
                         x1_ref, h2_ref, route_ref):
    x1 = _merge_body(n_seq_tiles, x_ref, u_ref, up_ref, fox_ref, mo_ref, g_ref, wg_ref, pw_ref,
                     ps_ref, wp_ref, wfb_ref, wmb_ref, wo_ref)
    x1_ref[...] = x1
    h2 = _rms(x1, g2_ref[...])
    _store_row_tiles(h2_ref, h2)

    h_hi, h_lo = _split_bf16(h2)
    w_hi, w_lo = _split_bf16(wr_ref[...])
    logits = _dot(h_hi, w_hi) + (_dot(h_hi, w_lo) + _dot(h_lo, w_hi)) + br_ref[...]
    lane = lax.broadcasted_iota(jnp.int32, logits.shape, 1)
    lane_f = lane.astype(F32)
    logits = jnp.where(lane < N_EXPERTS, logits, -jnp.inf)
    m1 = jnp.max(logits, axis=-1, keepdims=True)
    i1 = jnp.min(jnp.where(logits == m1, lane_f, float(LANES)), axis=-1, keepdims=True)
    rest = jnp.where(lane_f == i1, -jnp.inf, logits)
    m2 = jnp.max(rest, axis=-1, keepdims=True)
    i2 = jnp.min(jnp.where(rest == m2, lane_f, float(LANES)), axis=-1, keepdims=True)
    e = jnp.exp(m2 - m1)
    w1 = 1.0 / (1.0 + e)
    w2 = e / (1.0 + e)
    route = jnp.where(lane == 0, i1, 0.0)
    route = jnp.where(lane == 1, i2, route)
    route = jnp.where(lane == 2, w1, route)
    route_ref[...] = jnp.where(lane == 3, w2, route)


def _merge_specs(T, S, tm):
    D = D_MODEL
    tok = lambda i: (i, 0)
    halo_blocks = tm // POOL_HALO
    in_specs = [pl.BlockSpec((tm, D), tok),
                pl.BlockSpec((tm, POOL_WIDTH), tok),
                pl.BlockSpec((POOL_HALO, POOL_WIDTH), lambda i: (jnp.maximum(i * halo_blocks - 1, 0), 0)),
                pl.BlockSpec((tm, FOX_WIDTH), tok),
                pl.BlockSpec((tm, MEM_WIDTH), tok),
                _resident((1, D)), _resident((D, N_BRANCH * D)),
                _resident((len(POOL_WINDOWS), POOL_GROUP, POOL_GROUP)), _resident((1, POOL_WIDTH)),
                _resident((POOL_WIDTH, D)), _resident((FOX_WIDTH, D)), _resident((MEM_WIDTH, D)),
                _resident((D, D)), _resident((1, D))]
    return in_specs, tok


def _merge_ffn(x2d, S, u, fox_o, mem_o, g, wg, pw, ps, wp, wfb, wmb, wo, g2, wgu, wd):
    T, D = x2d.shape
    tm = TM_MERGE
    in_specs, tok = _merge_specs(T, S, tm)
    in_specs += [_resident((D, 2 * D_FF)), _resident((D_FF, D))]
    return pl.pallas_call(
        functools.partial(_merge_ffn_kernel, S // tm),
        grid=(T // tm,),
        in_specs=in_specs,
        out_specs=pl.BlockSpec((tm, D), tok),
        out_shape=jax.ShapeDtypeStruct((T, D), F32),
        scratch_shapes=[pltpu.VMEM((tm, D_FF), BF)],
        compiler_params=_params("arbitrary"),
        name="merge_ffn",
    )(x2d, u, u, fox_o, mem_o, g, wg, pw, ps, wp, wfb, wmb, wo, g2, wgu, wd)


def _merge_router(x2d, S, u, fox_o, mem_o, g, wg, pw, ps, wp, wfb, wmb, wo, g2, wr, br):
    T, D = x2d.shape
    tm = TM_MERGE
    in_specs, tok = _merge_specs(T, S, tm)
    in_specs += [_resident((D, LANES)), _resident((1, LANES))]
    return pl.pallas_call(
        functools.partial(_merge_router_kernel, S // tm),
        grid=(T // tm,),
        in_specs=in_specs,
        out_specs=[pl.BlockSpec((tm, D), tok), pl.BlockSpec((tm * ROW_TILE, LANES), tok),
                   pl.BlockSpec((tm, LANES), tok)],
        out_shape=[jax.ShapeDtypeStruct((T, D), F32), jax.ShapeDtypeStruct((T * ROW_TILE, LANES), F32),
                   jax.ShapeDtypeStruct((T, LANES), F32)],
        compiler_params=_params("arbitrary"),
        name="merge_router",
    )(x2d, u, u, fox_o, mem_o, g, wg, pw, ps, wp, wfb, wmb, wo, g2, wr, br)


def _row_tile(ref, r):
    start = r * ROW_TILE if isinstance(r, int) else pl.multiple_of(r * ROW_TILE, ROW_TILE)
    return ref.at[pl.ds(start, ROW_TILE)]


def _dispatch_kernel(pad_ref, pos_ref, zeros_ref, h_ref, xs_ref, sem):
    i = pl.program_id(0)
    n = pos_ref.shape[-1]
    pad_rows = zeros_ref.shape[0]

    def zero_fill(first_slot):
        dst = xs_ref.at[pl.ds(pl.multiple_of(first_slot * ROW_TILE, ROW_TILE), pad_rows)]
        fill = pltpu.make_async_copy(zeros_ref, dst, sem)
        fill.start()
        fill.wait()

    @pl.when(i == 0)
    def _():
        for e in range(N_EXPERTS):
            zero_fill(pad_ref[e])
        tile_slots = pad_rows // ROW_TILE

        def tail(t, carry):
            zero_fill(t * tile_slots)
            return carry

        lax.fori_loop(pad_ref[N_EXPERTS], xs_ref.shape[0] // pad_rows, tail, 0)

    def row_copy(t, slot):
        return pltpu.make_async_copy(_row_tile(h_ref, t), _row_tile(xs_ref, slot), sem)

    def issue(t, carry):
        for k in range(TOP_K):
            row_copy(t, pos_ref[0, 0, TOP_K * t + k]).start(priority=k % 2)
        return carry

    def drain(r, carry):
        row_copy(0, 0).wait()
        return carry

    lax.fori_loop(0, n // TOP_K, issue, 0, unroll=4)
    lax.fori_loop(0, n, drain, 0, unroll=8)


def _dispatch(h_tiles, pos, pad_start, n_slots):
    n = pos.shape[0]
    R = DISPATCH_ROWS
    grid_spec = pltpu.PrefetchScalarGridSpec(
        num_scalar_prefetch=1,
        grid=(n // R,),
        in_specs=[pl.BlockSpec((1, 1, R), lambda i, pad: (i, 0, 0), memory_space=pltpu.SMEM),
                  pl.BlockSpec(memory_space=pl.ANY),
                  pl.BlockSpec((R // TOP_K * ROW_TILE, LANES), lambda i, pad: (i, 0))],
        out_specs=pl.BlockSpec(memory_space=pl.ANY),
        scratch_shapes=[pltpu.SemaphoreType.DMA],
    )
    zeros = jnp.zeros((TM_EXPERT * ROW_TILE, LANES), F32)
    return pl.pallas_call(
        _dispatch_kernel,
        grid_spec=grid_spec,
        out_shape=jax.ShapeDtypeStruct((n_slots * ROW_TILE, LANES), F32),
        compiler_params=_params("arbitrary"),
        name="moe_dispatch",
    )(pad_start, pos.reshape(n // R, 1, R), zeros, h_tiles)


def _expert_kernel(te_ref, nu_ref, xs_ref, wgu_ref, wd_ref, y_ref, act_ref):
    i = pl.program_id(0)
    tm = act_ref.shape[0]

    @pl.when(i < nu_ref[0])
    def _():
        xb = _load_row_tiles(xs_ref, tm).astype(BF)
        _store_row_tiles(y_ref, _swiglu(xb, wgu_ref, wd_ref, act_ref, D_EXPERT))

    @pl.when(i >= nu_ref[0])
    def _():
        y_ref[...] = jnp.zeros_like(y_ref)


def _experts(xs, tile_expert, n_used, wgu, wd):
    D = D_MODEL
    tm = TM_EXPERT
    blk = tm * ROW_TILE
    used = lambda i, te, nu: (jnp.minimum(i, nu[0] - 1), 0)
    grid_spec = pltpu.PrefetchScalarGridSpec(
        num_scalar_prefetch=2,
        grid=(xs.shape[0] // blk,),
        in_specs=[pl.BlockSpec((blk, LANES), used),
                  pl.BlockSpec((None, D, 2 * D_EXPERT), lambda i, te, nu: (te[i], 0, 0)),
                  pl.BlockSpec((None, D_EXPERT, D), lambda i, te, nu: (te[i], 0, 0))],
        out_specs=pl.BlockSpec((blk, LANES), lambda i, te, nu: (i, 0)),
        scratch_shapes=[pltpu.VMEM((tm, D_EXPERT), BF)],
    )
    return pl.pallas_call(
        _expert_kernel,
        grid_spec=grid_spec,
        out_shape=jax.ShapeDtypeStruct(xs.shape, F32),
        compiler_params=_params("arbitrary"),
        name="experts",
    )(tile_expert, n_used, xs, wgu, wd)


def _combine_kernel(pos_ref, x1_ref, route_ref, y_ref, o_ref, yv_ref, sem):
    tm = x1_ref.shape[0]
    n = pos_ref.shape[-1]

    def row_copy(slot, r):
        return pltpu.make_async_copy(_row_tile(y_ref, slot), _row_tile(yv_ref, r), sem)

    def issue(t, carry):
        for k in range(TOP_K):
            r = TOP_K * t + k
            row_copy(pos_ref[0, 0, r], r).start(priority=k % 2)
        return carry

    def drain(r, carry):
        row_copy(0, 0).wait()
        return carry

    lax.fori_loop(0, n // TOP_K, issue, 0, unroll=4)
    lax.fori_loop(0, n, drain, 0, unroll=8)
    w = route_ref[...]
    y0 = _load_row_tiles(yv_ref, tm, 0, TOP_K * ROW_TILE)
    y1 = _load_row_tiles(yv_ref, tm, ROW_TILE, TOP_K * ROW_TILE)
    o_ref[...] = x1_ref[...] + w[:, 2:3] * y0 + w[:, 3:4] * y1


def _combine(x1, route, y_tiles, pos):
    T, D = x1.shape
    tm = TM_MERGE
    n = TOP_K * tm
    return pl.pallas_call(
        _combine_kernel,
        grid=(T // tm,),
        in_specs=[pl.BlockSpec((1, 1, n), lambda i: (i, 0, 0), memory_space=pltpu.SMEM),
                  pl.BlockSpec((tm, D), lambda i: (i, 0)),
                  pl.BlockSpec((tm, LANES), lambda i: (i, 0)),
                  pl.BlockSpec(memory_space=pl.ANY)],
        out_specs=pl.BlockSpec((tm, D), lambda i: (i, 0)),
        out_shape=jax.ShapeDtypeStruct((T, D), F32),
        scratch_shapes=[pltpu.VMEM((n * ROW_TILE, LANES), F32), pltpu.SemaphoreType.DMA],
        compiler_params=_params("arbitrary"),
        name="moe_combine",
    )(pos.reshape(T // tm, 1, n), x1, route, y_tiles)


def _dispatch_plan(route, T):
    tm = TM_EXPERT
    n_slots = TOP_K * T + (N_EXPERTS + 1) * tm
    n_tiles = n_slots // tm
    e = route[:, :TOP_K].astype(jnp.int32).reshape(-1)
    onehot = (e[:, None] == jnp.arange(N_EXPERTS, dtype=jnp.int32)[None, :]).astype(jnp.int32)
    csum = jnp.cumsum(onehot, axis=0)
    counts = csum[-1]
    rank = jnp.take_along_axis(csum, e[:, None], axis=1)[:, 0] - 1
    padded = ((counts + tm - 1) // tm) * tm
    ends = jnp.cumsum(padded)
    starts = ends - padded
    pos = (starts[e] + rank).astype(jnp.int32)
    tile_ends = ends // tm
    pad_start = jnp.concatenate([starts + counts, tile_ends[-1:]]).astype(jnp.int32)
    tile_ids = jnp.arange(n_tiles, dtype=jnp.int32)
    tile_expert = jnp.minimum(jnp.sum((tile_ids[:, None] >= tile_ends[None, :]).astype(jnp.int32), axis=1),
                              N_EXPERTS - 1)
    n_used = tile_ends[-1:].astype(jnp.int32)
    return pos, pad_start, tile_expert, n_used, n_slots


def _row(v, width=None):
    v = v.astype(F32).reshape(1, -1)
    if width is not None and v.shape[1] < width:
        v = jnp.pad(v, ((0, 0), (0, width - v.shape[1])))
    return v


def kernel(x, mem, mix_norm_g, w_in, b_forget, fox_q_g, fox_k_g, pool_w, pool_scale, mem_norm_g,
           w_mem_kv, mem_q_g, mem_k_g, w_pool_br, w_fox_br, w_mem_br, w_out, ffn_norm_g, w_ffn_gu,
           w_ffn_down, w_router, b_router, w_exp_gu, w_exp_down):
    B, S, D = x.shape
    T = B * S
    depth = w_in.shape[0]
    assert D == D_MODEL and S % TM_IN == 0 and S % TM_MERGE == 0 and mem.shape[1] == N_MEM
    nk = S // TQ
    head_mean = jnp.kron(jnp.eye(FOX_HEADS, dtype=F32),
                         jnp.full((FOX_HEAD_DIM, FOX_HEAD_DIM), 1.0 / FOX_HEAD_DIM, F32)).astype(BF)

    x2d = x.reshape(T, D)
    for layer in range(depth):
        w = w_in[layer]
        wa = jnp.concatenate([w[:, OFF_POOL:OFF_F], w[:, OFF_MQ:OFF_G]], axis=1).astype(BF)
        wf = jnp.pad(w[:, OFF_F:OFF_MQ], ((0, 0), (0, LANES - FOX_HEADS))).astype(BF)
        wg = w[:, OFF_G:].astype(BF)
        qg = _row(jnp.tile(fox_q_g[layer], FOX_HEADS) * (FOX_HEAD_DIM ** -0.5 * LOG2E))
        kg = _row(jnp.tile(fox_k_g[layer], FOX_HEADS))
        mqg = _row(mem_q_g[layer] * MEM_HEAD_DIM ** -0.5)

        mkT, mv = _mem_kv(mem, _row(mem_norm_g[layer]), w_mem_kv[layer].astype(BF), _row(mem_k_g[layer]))
        u, q, kT, v, logf, mem_o = _in_proj(
            x2d, B, S, _row(mix_norm_g[layer]), wa, wf, _row(b_forget[layer], LANES), qg, kg,
            head_mean, mqg, mkT, mv)
        score_bound = (NORM_SLACK * FOX_HEAD_DIM * jnp.max(jnp.abs(qg)) * jnp.max(jnp.abs(kg)))
        fox_o = _fox(score_bound, q, _cumsum(logf, kT), v).reshape(T, FOX_WIDTH)

        merge_args = (x2d, S, u, fox_o, mem_o, _row(mix_norm_g[layer]), wg, pool_w[layer].astype(BF),
                      _row(pool_scale[layer]), w_pool_br[layer].astype(BF), w_fox_br[layer].astype(BF),
                      w_mem_br[layer].astype(BF), w_out[layer].astype(BF), _row(ffn_norm_g[layer]))
        if layer % 2 == 0:
            x2d = _merge_ffn(*merge_args, w_ffn_gu[layer // 2].astype(BF), w_ffn_down[layer // 2].astype(BF))
        else:
            m = layer // 2
            wr = jnp.pad(w_router[m], ((0, 0), (0, LANES - N_EXPERTS)))
            x1, h2, route = _merge_router(*merge_args, wr, _row(b_router[m], LANES))
            pos, pad_start, tile_expert, n_used, n_slots = _dispatch_plan(route, T)
            xs = _dispatch(h2, pos, pad_start, n_slots)
            y = _experts(xs, tile_expert, n_used, w_exp_gu[m].astype(BF), w_exp_down[m].astype(BF))
            x2d = _combine(x1, route, y, pos)
    return x2d.reshape(B, S, D)
```

```python
import functools

import jax
import jax.numpy as jnp
from jax import lax
from jax.experimental import pallas as pl
from jax.experimental.pallas import tpu as pltpu

D_MODEL = 1024
N_MEM = 256
POOL_WINDOWS = (2, 4, 8, 16)
POOL_WIDTH = 512
POOL_GROUP = 128
POOL_HALO = 16
FOX_HEADS = 8
FOX_HEAD_DIM = 64
FOX_WIDTH = 512
MEM_HEADS = 4
MEM_HEAD_DIM = 128
MEM_WIDTH = 512
N_BRANCH = 3
D_FF = 2816
N_EXPERTS = 8
TOP_K = 2
D_EXPERT = 3584
EPS = 1e-6

OFF_POOL = 0
OFF_Q = OFF_POOL + POOL_WIDTH
OFF_K = OFF_Q + FOX_WIDTH
OFF_V = OFF_K + FOX_WIDTH
OFF_F = OFF_V + FOX_WIDTH
OFF_MQ = OFF_F + FOX_HEADS
OFF_G = OFF_MQ + MEM_WIDTH

LANES = 128
LOG2E = 1.4426950408889634
BIAS_TERMS = 3
BIAS_ROWS = 16
SAFE_EXP2_SPAN = 100.0
NORM_SLACK = 1.02
BF = jnp.bfloat16
F32 = jnp.float32

TM_IN = 512
TQ = 512
TM_MERGE = 512
TM_EXPERT = 512
DISPATCH_ROWS = 1024
FF_CHUNK = 256
VMEM_LIMIT = 58 * 1024 * 1024


def _resident(shape):
    nd = len(shape)
    return pl.BlockSpec(shape, lambda *_: (0,) * nd, pipeline_mode=pl.Buffered(1))


def _params(*sem):
    return pltpu.CompilerParams(dimension_semantics=sem, vmem_limit_bytes=VMEM_LIMIT)


def _dot(a, b):
    return jnp.dot(a, b, preferred_element_type=F32)


def _rms(x, g):
    ms = jnp.mean(x * x, axis=-1, keepdims=True)
    return x * lax.rsqrt(ms + EPS) * g


def _sigmoid(z):
    return 1.0 / (1.0 + jnp.exp(-z))


def _mem_kv_kernel(mem_ref, g_ref, w_ref, kg_ref, kT_ref, v_ref):
    mn = _rms(mem_ref[...], g_ref[...]).astype(BF)
    kv = _dot(mn, w_ref[...])
    for h in range(MEM_HEADS):
        kh = kv[:, h * MEM_HEAD_DIM:(h + 1) * MEM_HEAD_DIM]
        kT_ref[h] = _rms(kh, kg_ref[...]).T.astype(BF)
    v_ref[...] = kv[:, MEM_WIDTH:].astype(BF)


def _mem_kv(mem, g, w_kv, k_g):
    B, M, D = mem.shape
    return pl.pallas_call(
        _mem_kv_kernel,
        grid=(B,),
        in_specs=[pl.BlockSpec((None, M, D), lambda b: (b, 0, 0)),
                  _resident((1, D)), _resident((D, 2 * MEM_WIDTH)), _resident((1, MEM_HEAD_DIM))],
        out_specs=[pl.BlockSpec((None, MEM_HEADS, MEM_HEAD_DIM, M), lambda b: (b, 0, 0, 0)),
                   pl.BlockSpec((None, M, MEM_WIDTH), lambda b: (b, 0, 0))],
        out_shape=[jax.ShapeDtypeStruct((B, MEM_HEADS, MEM_HEAD_DIM, M), BF),
                   jax.ShapeDtypeStruct((B, M, MEM_WIDTH), BF)],
        compiler_params=_params("arbitrary"),
        name="mem_kv",
    )(mem, g, w_kv, k_g)


def _in_proj_kernel(x_ref, g_ref, wa_ref, wf_ref, bf_ref, qg_ref, kg_ref, bd_ref, mqg_ref,
                    mkT_ref, mv_ref, u_ref, q_ref, kT_ref, v_ref, lf_ref, mo_ref):
    tm = x_ref.shape[0]
    hb = _rms(x_ref[...], g_ref[...]).astype(BF)

    u_ref[...] = _dot(hb, wa_ref[:, OFF_POOL:OFF_Q]).astype(BF)

    lane = lax.broadcasted_iota(jnp.int32, (tm, LANES), 1)
    live = lane < FOX_HEAD_DIM

    def split_heads(z, n_ones, ref):
        pad = jnp.where(lane < FOX_HEAD_DIM + n_ones, 1.0, 0.0)
        for p in range(FOX_HEADS // 2):
            pair = z[:, p * LANES:(p + 1) * LANES]
            ref[2 * p] = jnp.where(live, pair, pad).astype(BF)
            ref[2 * p + 1] = jnp.where(live, pltpu.roll(pair, FOX_HEAD_DIM, axis=1), pad).astype(BF)

    bd = bd_ref[...]
    zq = _dot(hb, wa_ref[:, OFF_Q:OFF_K])
    qms = _dot((zq * zq).astype(BF), bd)
    split_heads(zq * lax.rsqrt(qms + EPS) * qg_ref[...], BIAS_TERMS, q_ref)
    zk = _dot(hb, wa_ref[:, OFF_K:OFF_V])
    kms = _dot((zk * zk).astype(BF), bd)
    kn = zk * lax.rsqrt(kms + EPS) * kg_ref[...]
    kT_ref[:, 0:FOX_HEAD_DIM, :] = kn.T.reshape(FOX_HEADS, FOX_HEAD_DIM, tm).astype(BF)
    kT_ref[:, FOX_HEAD_DIM:, :] = jnp.zeros((FOX_HEADS, LANES - FOX_HEAD_DIM, tm), BF)
    split_heads(_dot(hb, wa_ref[:, OFF_V:OFF_F]), 1, v_ref)

    zf = _dot(hb, wf_ref[...]) + bf_ref[...]
    logf = jnp.minimum(zf, 0.0) - jnp.log(1.0 + jnp.exp(-jnp.abs(zf)))
    lf_ref[...] = logf.T[0:FOX_HEADS, :]

    zm = _dot(hb, wa_ref[:, OFF_F:OFF_F + MEM_WIDTH])
    for h in range(MEM_HEADS):
        sl = slice(h * MEM_HEAD_DIM, (h + 1) * MEM_HEAD_DIM)
        qn = _rms(zm[:, sl], mqg_ref[...]).astype(BF)
        s = _dot(qn, mkT_ref[h])
        p = jnp.exp(s - jnp.max(s, axis=-1, keepdims=True))
        l = jnp.sum(p, axis=-1, keepdims=True)
        mo_ref[:, sl] = (_dot(p.astype(BF), mv_ref[:, sl]) / l).astype(BF)


def _in_proj(x2d, B, S, g, wa, wf, bfg, qg, kg, bd, mqg, mkT, mv):
    T, D = x2d.shape
    tm = TM_IN
    nk = S // tm
    wa_cols = wa.shape[1]
    M = mkT.shape[-1]
    tok = lambda i: (i, 0)
    return pl.pallas_call(
        _in_proj_kernel,
        grid=(T // tm,),
        in_specs=[pl.BlockSpec((tm, D), tok),
                  _resident((1, D)), _resident((D, wa_cols)), _resident((D, LANES)),
                  _resident((1, LANES)), _resident((1, FOX_WIDTH)), _resident((1, FOX_WIDTH)),
                  _resident((FOX_WIDTH, FOX_WIDTH)), _resident((1, MEM_HEAD_DIM)),
                  pl.BlockSpec((None, MEM_HEADS, MEM_HEAD_DIM, M), lambda i: (i // nk, 0, 0, 0)),
                  pl.BlockSpec((None, M, MEM_WIDTH), lambda i: (i // nk, 0, 0))],
        out_specs=[pl.BlockSpec((tm, POOL_WIDTH), tok),
                   pl.BlockSpec((None, FOX_HEADS, tm, LANES), lambda i: (i // nk, 0, i % nk, 0)),
                   pl.BlockSpec((None, FOX_HEADS, None, LANES, tm), lambda i: (i // nk, 0, i % nk, 0, 0)),
                   pl.BlockSpec((None, FOX_HEADS, tm, LANES), lambda i: (i // nk, 0, i % nk, 0)),
                   pl.BlockSpec((None, FOX_HEADS, tm), lambda i: (i // nk, 0, i % nk)),
                   pl.BlockSpec((tm, MEM_WIDTH), tok)],
        out_shape=[jax.ShapeDtypeStruct((T, POOL_WIDTH), BF),
                   jax.ShapeDtypeStruct((B, FOX_HEADS, S, LANES), BF),
                   jax.ShapeDtypeStruct((B, FOX_HEADS, nk, LANES, tm), BF),
                   jax.ShapeDtypeStruct((B, FOX_HEADS, S, LANES), BF),
                   jax.ShapeDtypeStruct((B, FOX_HEADS, S), F32),
                   jax.ShapeDtypeStruct((T, MEM_WIDTH), BF)],
        compiler_params=_params("arbitrary"),
        name="in_proj",
    )(x2d, g, wa, wf, bfg, qg, kg, bd, mqg, mkT, mv)


def _split_terms(x, n):
    terms = []
    for _ in range(n):
        t = x.astype(BF).astype(F32)
        terms.append(t)
        x = x - t
    return terms


def _cumsum_kernel(lf_ref, k_any_ref, krow_ref):
    del k_any_ref
    x = lf_ref[...]
    H, S = x.shape
    tk = krow_ref.shape[-1]
    lane = lax.broadcasted_iota(jnp.int32, x.shape, 1)
    k = 1
    while k < S:
        x = x + jnp.where(lane >= k, pltpu.roll(x, k, axis=1), 0.0)
        k *= 2
    terms = _split_terms(x * (-LOG2E), BIAS_TERMS)
    sub = lax.broadcasted_iota(jnp.int32, (BIAS_ROWS, tk), 0)
    ones_rows = jnp.where((sub >= BIAS_TERMS) & (sub < 2 * BIAS_TERMS), 1.0, 0.0)
    for j in range(S // tk):
        sl = slice(j * tk, (j + 1) * tk)
        for h in range(H):
            blk = ones_rows
            for r, t in enumerate(terms):
                blk = jnp.where(sub == r, t[h:h + 1, sl], blk)
            krow_ref[h, j] = blk.astype(BF)


def _cumsum(logf, kT):
    B, H, S = logf.shape
    nk, tk = kT.shape[2], kT.shape[4]
    assert FOX_HEAD_DIM % BIAS_ROWS == 0 and 2 * BIAS_TERMS <= BIAS_ROWS
    return pl.pallas_call(
        _cumsum_kernel, grid=(B,),
        in_specs=[pl.BlockSpec((None, H, S), lambda b: (b, 0, 0)),
                  pl.BlockSpec(memory_space=pl.ANY)],
        out_specs=pl.BlockSpec((None, H, nk, BIAS_ROWS, tk),
                               lambda b: (b, 0, 0, FOX_HEAD_DIM // BIAS_ROWS, 0)),
        out_shape=jax.ShapeDtypeStruct(kT.shape, kT.dtype),
        input_output_aliases={1: 0},
        compiler_params=_params("arbitrary"), name="forget_cumsum",
    )(logf, kT)


def _fox_kernel(exact_max, ub_ref, q_ref, kT_ref, v_ref, o_ref):
    tq = q_ref.shape[1]
    tk = kT_ref.shape[-1]
    i = pl.program_id(2)
    row = lax.broadcasted_iota(jnp.int32, (tq, tk), 0)
    col = lax.broadcasted_iota(jnp.int32, (tq, tk), 1)
    causal = col <= row
    lane = lax.broadcasted_iota(jnp.int32, (tq, LANES), 1)

    def scores(q, h, j, diagonal):
        z = _dot(q, kT_ref[h, j])
        return jnp.where(causal, z, -jnp.inf) if diagonal else z

    def max_step(j, carry, diagonal):
        out = []
        for h in range(2):
            z = scores(q_ref[h], h, j, diagonal)
            m = carry[h]
            for c in range(tk // LANES):
                m = jnp.maximum(m, z[:, c * LANES:(c + 1) * LANES])
            out.append(m)
        return tuple(out)

    if exact_max:
        m_lane = tuple(jnp.full((tq, LANES), -jnp.inf, F32) for _ in range(2))
        m_lane = max_step(i, lax.fori_loop(0, i, lambda j, c: max_step(j, c, False), m_lane), True)
    else:
        sub = lax.broadcasted_iota(jnp.int32, (LANES, LANES), 0)
        pick = jnp.where((sub >= FOX_HEAD_DIM) & (sub < FOX_HEAD_DIM + BIAS_TERMS), 1.0, 0.0).astype(BF)
        m_lane = tuple(lax.dot_general(kT_ref[h, i], pick, (((0,), (0,)), ((), ())),
                                       preferred_element_type=F32) + ub_ref[0] for h in range(2))
    q_stab = []
    for h in range(2):
        m_row = jnp.max(m_lane[h], axis=-1, keepdims=True)
        q = q_ref[h]
        for r, t in enumerate(_split_terms(-m_row, BIAS_TERMS)):
            q = jnp.where(lane == FOX_HEAD_DIM + BIAS_TERMS + r, t.astype(BF), q)
        q_stab.append(q)

    def pv_step(j, carry, diagonal):
        rows = pl.ds(pl.multiple_of(j * tk, tk), tk)
        out = []
        for h in range(2):
            p = jnp.exp2(scores(q_stab[h], h, j, diagonal))
            out.append(carry[h] + _dot(p.astype(BF), v_ref[h, rows, :]))
        return tuple(out)

    def pv_diagonal(carry):
        half = tq // 2
        kv_rows = pl.ds(pl.multiple_of(i * tk, tk), tk)
        out = []
        for h in range(2):
            kT = kT_ref[h, i]
            v = v_ref[h, kv_rows, :]
            z_top = jnp.where(causal[:half, :half], _dot(q_stab[h][:half], kT[:, :half]), -jnp.inf)
            z_bot = jnp.where(causal[half:, :], _dot(q_stab[h][half:], kT), -jnp.inf)
            a_top = _dot(jnp.exp2(z_top).astype(BF), v[:half])
            a_bot = _dot(jnp.exp2(z_bot).astype(BF), v)
            out.append(carry[h] + jnp.concatenate([a_top, a_bot], axis=0))
        return tuple(out)

    a_init = tuple(jnp.zeros((tq, LANES), F32) for _ in range(2))
    a0, a1 = pv_diagonal(lax.fori_loop(0, i, lambda j, c: pv_step(j, c, False), a_init))
    o0 = a0 / a0[:, FOX_HEAD_DIM:FOX_HEAD_DIM + 1]
    o1 = a1 / a1[:, FOX_HEAD_DIM:FOX_HEAD_DIM + 1]
    o_ref[...] = jnp.where(lane < FOX_HEAD_DIM, o0, pltpu.roll(o1, FOX_HEAD_DIM, axis=1)).astype(BF)


def _fox_call(exact_max, ub, q, kT, v):
    B, H, S, _ = q.shape
    nk = S // TQ
    grid_spec = pltpu.PrefetchScalarGridSpec(
        num_scalar_prefetch=1,
        grid=(B, H // 2, nk),
        in_specs=[pl.BlockSpec((None, 2, TQ, LANES), lambda b, h, i, ub: (b, h, i, 0)),
                  pl.BlockSpec((None, 2, nk, LANES, TQ), lambda b, h, i, ub: (b, h, 0, 0, 0)),
                  pl.BlockSpec((None, 2, S, LANES), lambda b, h, i, ub: (b, h, 0, 0))],
        out_specs=pl.BlockSpec((None, TQ, LANES), lambda b, h, i, ub: (b, i, h)),
    )
    return pl.pallas_call(
        functools.partial(_fox_kernel, exact_max),
        grid_spec=grid_spec,
        out_shape=jax.ShapeDtypeStruct((B, S, H * FOX_HEAD_DIM), BF),
        compiler_params=_params("arbitrary", "arbitrary", "arbitrary"),
        name="fox_attention_exact_max" if exact_max else "fox_attention",
    )(ub, q, kT, v)


def _fox(score_bound, q, kT, v):
    ub = score_bound.reshape(1).astype(F32)
    return lax.cond(2.0 * score_bound <= SAFE_EXP2_SPAN,
                    functools.partial(_fox_call, False), functools.partial(_fox_call, True),
                    ub, q, kT, v)


def _merge_body(n_seq_tiles, x_ref, u_ref, up_ref, fox_ref, mo_ref, g_ref, wg_ref, pw_ref, ps_ref,
                wp_ref, wfb_ref, wmb_ref, wo_ref):
    tm = x_ref.shape[0]
    i = pl.program_id(0)
    x = x_ref[...]
    hb = _rms(x, g_ref[...]).astype(BF)

    seq_tile = i % n_seq_tiles
    halo = jnp.where(seq_tile == 0, 0.0, up_ref[...].astype(F32))
    u = u_ref[...].astype(F32)
    ext = jnp.concatenate([halo, u], axis=0)
    pos = seq_tile * tm + lax.broadcasted_iota(jnp.int32, (tm, 1), 0)
    outs = []
    for g, w in enumerate(POOL_WINDOWS):
        sl = slice(g * POOL_GROUP, (g + 1) * POOL_GROUP)
        s = ext[:, sl]
        k = 1
        while k < w:
            s = s + pltpu.roll(s, k, axis=0)
            k *= 2
        cnt = jnp.minimum(pos + 1, w).astype(F32)
        d = s[POOL_HALO:, :] / cnt - u[:, sl]
        outs.append(_dot(d.astype(BF), pw_ref[g]))
    pool_o = (jnp.concatenate(outs, axis=1) * ps_ref[...]).astype(BF)

    merged = _sigmoid(_dot(hb, wg_ref[:, 0:D_MODEL])) * _dot(pool_o, wp_ref[...])
    merged += _sigmoid(_dot(hb, wg_ref[:, D_MODEL:2 * D_MODEL])) * _dot(fox_ref[...], wfb_ref[...])
    merged += _sigmoid(_dot(hb, wg_ref[:, 2 * D_MODEL:])) * _dot(mo_ref[...], wmb_ref[...])
    return x + _dot(merged.astype(BF), wo_ref[...])


def _swiglu(hb, wgu_ref, wd_ref, act_ref, d_hidden):
    for c in range(d_hidden // FF_CHUNK):
        lo = c * FF_CHUNK
        g = _dot(hb, wgu_ref[:, lo:lo + FF_CHUNK])
        up = _dot(hb, wgu_ref[:, d_hidden + lo:d_hidden + lo + FF_CHUNK])
        act_ref[:, lo:lo + FF_CHUNK] = (g * _sigmoid(g) * up).astype(BF)
    return _dot(act_ref[...], wd_ref[...])


def _merge_ffn_kernel(n_seq_tiles, x_ref, u_ref, up_ref, fox_ref, mo_ref, g_ref, wg_ref, pw_ref,
                      ps_ref, wp_ref, wfb_ref, wmb_ref, wo_ref, g2_ref, wgu_ref, wd_ref,
                      o_ref, act_ref):
    x1 = _merge_body(n_seq_tiles, x_ref, u_ref, up_ref, fox_ref, mo_ref, g_ref, wg_ref, pw_ref,
                     ps_ref, wp_ref, wfb_ref, wmb_ref, wo_ref)
    h2 = _rms(x1, g2_ref[...]).astype(BF)
    o_ref[...] = x1 + _swiglu(h2, wgu_ref, wd_ref, act_ref, D_FF)


def _split_bf16(a):
    hi = a.astype(BF)
    return hi, (a - hi.astype(F32)).astype(BF)


ROW_TILE = D_MODEL // LANES


def _store_row_tiles(ref, x, row0=0, stride=ROW_TILE):
    n = x.shape[0]
    for c in range(ROW_TILE):
        ref[pl.ds(row0 + c, n, stride=stride), :] = x[:, c * LANES:(c + 1) * LANES]


def _load_row_tiles(ref, n, row0=0, stride=ROW_TILE):
    return jnp.concatenate([ref[pl.ds(row0 + c, n, stride=stride), :] for c in range(ROW_TILE)], axis=1)


def _merge_router_kernel(n_seq_tiles, x_ref, u_ref, up_ref, fox_ref, mo_ref, g_ref, wg_ref, pw_ref,
                         ps_ref, wp_ref, wfb_ref, wmb_ref, wo_ref, g2_ref, wr_ref, br_ref,
                         x1_ref, h2_ref, route_ref):
    x1 = _merge_body(n_seq_tiles, x_ref, u_ref, up_ref, fox_ref, mo_ref, g_ref, wg_ref, pw_ref,
                     ps_ref, wp_ref, wfb_ref, wmb_ref, wo_ref)
    x1_ref[...] = x1
    h2 = _rms(x1, g2_ref[...])
    _store_row_tiles(h2_ref, h2)

    h_hi, h_lo = _split_bf16(h2)
    w_hi, w_lo = _split_bf16(wr_ref[...])
    logits = _dot(h_hi, w_hi) + (_dot(h_hi, w_lo) + _dot(h_lo, w_hi)) + br_ref[...]
    lane = lax.broadcasted_iota(jnp.int32, logits.shape, 1)
    lane_f = lane.astype(F32)
    logits = jnp.where(lane < N_EXPERTS, logits, -jnp.inf)
    m1 = jnp.max(logits, axis=-1, keepdims=True)
    i1 = jnp.min(jnp.where(logits == m1, lane_f, float(LANES)), axis=-1, keepdims=True)
    rest = jnp.where(lane_f == i1, -jnp.inf, logits)
    m2 = jnp.max(rest, axis=-1, keepdims=True)
    i2 = jnp.min(jnp.where(rest == m2, lane_f, float(LANES)), axis=-1, keepdims=True)
    e = jnp.exp(m2 - m1)
    w1 = 1.0 / (1.0 + e)
    w2 = e / (1.0 + e)
    route = jnp.where(lane == 0, i1, 0.0)
    route = jnp.where(lane == 1, i2, route)
    route = jnp.where(lane == 2, w1, route)
    route_ref[...] = jnp.where(lane == 3, w2, route)


def _merge_specs(T, S, tm):
    D = D_MODEL
    tok = lambda i: (i, 0)
    halo_blocks = tm // POOL_HALO
    in_specs = [pl.BlockSpec((tm, D), tok),
                pl.BlockSpec((tm, POOL_WIDTH), tok),
                pl.BlockSpec((POOL_HALO, POOL_WIDTH), lambda i: (jnp.maximum(i * halo_blocks - 1, 0), 0)),
                pl.BlockSpec((tm, FOX_WIDTH), tok),
                pl.BlockSpec((tm, MEM_WIDTH), tok),
                _resident((1, D)), _resident((D, N_BRANCH * D)),
                _resident((len(POOL_WINDOWS), POOL_GROUP, POOL_GROUP)), _resident((1, POOL_WIDTH)),
                _resident((POOL_WIDTH, D)), _resident((FOX_WIDTH, D)), _resident((MEM_WIDTH, D)),
                _resident((D, D)), _resident((1, D))]
    return in_specs, tok


def _merge_ffn(x2d, S, u, fox_o, mem_o, g, wg, pw, ps, wp, wfb, wmb, wo, g2, wgu, wd):
    T, D = x2d.shape
    tm = TM_MERGE
    in_specs, tok = _merge_specs(T, S, tm)
    in_specs += [_resident((D, 2 * D_FF)), _resident((D_FF, D))]
    return pl.pallas_call(
        functools.partial(_merge_ffn_kernel, S // tm),
        grid=(T // tm,),
        in_specs=in_specs,
        out_specs=pl.BlockSpec((tm, D), tok),
        out_shape=jax.ShapeDtypeStruct((T, D), F32),
        scratch_shapes=[pltpu.VMEM((tm, D_FF), BF)],
        compiler_params=_params("arbitrary"),
        name="merge_ffn",
    )(x2d, u, u, fox_o, mem_o, g, wg, pw, ps, wp, wfb, wmb, wo, g2, wgu, wd)


def _merge_router(x2d, S, u, fox_o, mem_o, g, wg, pw, ps, wp, wfb, wmb, wo, g2, wr, br):
    T, D = x2d.shape
    tm = TM_MERGE
    in_specs, tok = _merge_specs(T, S, tm)
    in_specs += [_resident((D, LANES)), _resident((1, LANES))]
    return pl.pallas_call(
        functools.partial(_merge_router_kernel, S // tm),
        grid=(T // tm,),
        in_specs=in_specs,
        out_specs=[pl.BlockSpec((tm, D), tok), pl.BlockSpec((tm * ROW_TILE, LANES), tok),
                   pl.BlockSpec((tm, LANES), tok)],
        out_shape=[jax.ShapeDtypeStruct((T, D), F32), jax.ShapeDtypeStruct((T * ROW_TILE, LANES), F32),
                   jax.ShapeDtypeStruct((T, LANES), F32)],
        compiler_params=_params("arbitrary"),
        name="merge_router",
    )(x2d, u, u, fox_o, mem_o, g, wg, pw, ps, wp, wfb, wmb, wo, g2, wr, br)


def _row_tile(ref, r):
    start = r * ROW_TILE if isinstance(r, int) else pl.multiple_of(r * ROW_TILE, ROW_TILE)
    return ref.at[pl.ds(start, ROW_TILE)]


def _dispatch_kernel(pad_ref, pos_ref, zeros_ref, h_ref, xs_ref, sem):
    i = pl.program_id(0)
    n = pos_ref.shape[-1]
    pad_rows = zeros_ref.shape[0]

    def zero_fill(first_slot):
        dst = xs_ref.at[pl.ds(pl.multiple_of(first_slot * ROW_TILE, ROW_TILE), pad_rows)]
        fill = pltpu.make_async_copy(zeros_ref, dst, sem)
        fill.start()
        fill.wait()

    @pl.when(i == 0)
    def _():
        for e in range(N_EXPERTS):
            zero_fill(pad_ref[e])
        tile_slots = pad_rows // ROW_TILE

        def tail(t, carry):
            zero_fill(t * tile_slots)
            return carry

        lax.fori_loop(pad_ref[N_EXPERTS], xs_ref.shape[0] // pad_rows, tail, 0)

    def row_copy(t, slot):
        return pltpu.make_async_copy(_row_tile(h_ref, t), _row_tile(xs_ref, slot), sem)

    def issue(t, carry):
        for k in range(TOP_K):
            row_copy(t, pos_ref[0, 0, TOP_K * t + k]).start(priority=k % 2)
        return carry

    def drain(r, carry):
        row_copy(0, 0).wait()
        return carry

    lax.fori_loop(0, n // TOP_K, issue, 0, unroll=4)
    lax.fori_loop(0, n, drain, 0, unroll=8)


def _dispatch(h_tiles, pos, pad_start, n_slots):
    n = pos.shape[0]
    R = DISPATCH_ROWS
    grid_spec = pltpu.PrefetchScalarGridSpec(
        num_scalar_prefetch=1,
        grid=(n // R,),
        in_specs=[pl.BlockSpec((1, 1, R), lambda i, pad: (i, 0, 0), memory_space=pltpu.SMEM),
                  pl.BlockSpec(memory_space=pl.ANY),
                  pl.BlockSpec((R // TOP_K * ROW_TILE, LANES), lambda i, pad: (i, 0))],
        out_specs=pl.BlockSpec(memory_space=pl.ANY),
        scratch_shapes=[pltpu.SemaphoreType.DMA],
    )
    zeros = jnp.zeros((TM_EXPERT * ROW_TILE, LANES), F32)
    return pl.pallas_call(
        _dispatch_kernel,
        grid_spec=grid_spec,
        out_shape=jax.ShapeDtypeStruct((n_slots * ROW_TILE, LANES), F32),
        compiler_params=_params("arbitrary"),
        name="moe_dispatch",
    )(pad_start, pos.reshape(n // R, 1, R), zeros, h_tiles)


def _expert_kernel(te_ref, nu_ref, xs_ref, wgu_ref, wd_ref, y_ref, act_ref):
    i = pl.program_id(0)
    tm = act_ref.shape[0]

    @pl.when(i < nu_ref[0])
    def _():
        xb = _load_row_tiles(xs_ref, tm).astype(BF)
        _store_row_tiles(y_ref, _swiglu(xb, wgu_ref, wd_ref, act_ref, D_EXPERT))

    @pl.when(i >= nu_ref[0])
    def _():
        y_ref[...] = jnp.zeros_like(y_ref)


def _experts(xs, tile_expert, n_used, wgu, wd):
    D = D_MODEL
    tm = TM_EXPERT
    blk = tm * ROW_TILE
    used = lambda i, te, nu: (jnp.minimum(i, nu[0] - 1), 0)
    grid_spec = pltpu.PrefetchScalarGridSpec(
        num_scalar_prefetch=2,
        grid=(xs.shape[0] // blk,),
        in_specs=[pl.BlockSpec((blk, LANES), used),
                  pl.BlockSpec((None, D, 2 * D_EXPERT), lambda i, te, nu: (te[i], 0, 0),
                               pipeline_mode=pl.Buffered(1)),
                  pl.BlockSpec((None, D_EXPERT, D), lambda i, te, nu: (te[i], 0, 0),
                               pipeline_mode=pl.Buffered(1))],
        out_specs=pl.BlockSpec((blk, LANES), lambda i, te, nu: (i, 0)),
        scratch_shapes=[pltpu.VMEM((tm, D_EXPERT), BF)],
    )
    return pl.pallas_call(
        _expert_kernel,
        grid_spec=grid_spec,
        out_shape=jax.ShapeDtypeStruct(xs.shape, F32),
        compiler_params=_params("arbitrary"),
        name="experts",
    )(tile_expert, n_used, xs, wgu, wd)


def _combine_kernel(pos_ref, x1_ref, route_ref, y_ref, o_ref, yv_ref, sem):
    tm = x1_ref.shape[0]
    n = pos_ref.shape[-1]

    def row_copy(slot, r):
        return pltpu.make_async_copy(_row_tile(y_ref, slot), _row_tile(yv_ref, r), sem)

    def issue(t, carry):
        for k in range(TOP_K):
            r = TOP_K * t + k
            row_copy(pos_ref[0, 0, r], r).start(priority=k % 2)
        return carry

    def drain(r, carry):
        row_copy(0, 0).wait()
        return carry

    lax.fori_loop(0, n // TOP_K, issue, 0, unroll=4)
    lax.fori_loop(0, n, drain, 0, unroll=8)
    w = route_ref[...]
    y0 = _load_row_tiles(yv_ref, tm, 0, TOP_K * ROW_TILE)
    y1 = _load_row_tiles(yv_ref, tm, ROW_TILE, TOP_K * ROW_TILE)
    o_ref[...] = x1_ref[...] + w[:, 2:3] * y0 + w[:, 3:4] * y1


def _combine(x1, route, y_tiles, pos):
    T, D = x1.shape
    tm = TM_MERGE
    n = TOP_K * tm
    return pl.pallas_call(
        _combine_kernel,
        grid=(T // tm,),
        in_specs=[pl.BlockSpec((1, 1, n), lambda i: (i, 0, 0), memory_space=pltpu.SMEM),
                  pl.BlockSpec((tm, D), lambda i: (i, 0)),
                  pl.BlockSpec((tm, LANES), lambda i: (i, 0)),
                  pl.BlockSpec(memory_space=pl.ANY)],
        out_specs=pl.BlockSpec((tm, D), lambda i: (i, 0)),
        out_shape=jax.ShapeDtypeStruct((T, D), F32),
        scratch_shapes=[pltpu.VMEM((n * ROW_TILE, LANES), F32), pltpu.SemaphoreType.DMA],
        compiler_params=_params("arbitrary"),
        name="moe_combine",
    )(pos.reshape(T // tm, 1, n), x1, route, y_tiles)


def _dispatch_plan(route, T):
    tm = TM_EXPERT
    n_slots = TOP_K * T + (N_EXPERTS + 1) * tm
    n_tiles = n_slots // tm
    e = route[:, :TOP_K].astype(jnp.int32).reshape(-1)
    onehot = (e[:, None] == jnp.arange(N_EXPERTS, dtype=jnp.int32)[None, :]).astype(jnp.int32)
    csum = jnp.cumsum(onehot, axis=0)
    counts = csum[-1]
    rank = jnp.take_along_axis(csum, e[:, None], axis=1)[:, 0] - 1
    padded = ((counts + tm - 1) // tm) * tm
    ends = jnp.cumsum(padded)
    starts = ends - padded
    pos = (starts[e] + rank).astype(jnp.int32)
    tile_ends = ends // tm
    pad_start = jnp.concatenate([starts + counts, tile_ends[-1:]]).astype(jnp.int32)
    tile_ids = jnp.arange(n_tiles, dtype=jnp.int32)
    tile_expert = jnp.minimum(jnp.sum((tile_ids[:, None] >= tile_ends[None, :]).astype(jnp.int32), axis=1),
                              N_EXPERTS - 1)
    n_used = tile_ends[-1:].astype(jnp.int32)
    return pos, pad_start, tile_expert, n_used, n_slots


def _row(v, width=None):
    v = v.astype(F32).reshape(1, -1)
    if width is not None and v.shape[1] < width:
        v = jnp.pad(v, ((0, 0), (0, width - v.shape[1])))
    return v


def kernel(x, mem, mix_norm_g, w_in, b_forget, fox_q_g, fox_k_g, pool_w, pool_scale, mem_norm_g,
           w_mem_kv, mem_q_g, mem_k_g, w_pool_br, w_fox_br, w_mem_br, w_out, ffn_norm_g, w_ffn_gu,
           w_ffn_down, w_router, b_router, w_exp_gu, w_exp_down):
    B, S, D = x.shape
    T = B * S
    depth = w_in.shape[0]
    assert D == D_MODEL and S % TM_IN == 0 and S % TM_MERGE == 0 and mem.shape[1] == N_MEM
    nk = S // TQ
    head_mean = jnp.kron(jnp.eye(FOX_HEADS, dtype=F32),
                         jnp.full((FOX_HEAD_DIM, FOX_HEAD_DIM), 1.0 / FOX_HEAD_DIM, F32)).astype(BF)

    x2d = x.reshape(T, D)
    for layer in range(depth):
        w = w_in[layer]
        wa = jnp.concatenate([w[:, OFF_POOL:OFF_F], w[:, OFF_MQ:OFF_G]], axis=1).astype(BF)
        wf = jnp.pad(w[:, OFF_F:OFF_MQ], ((0, 0), (0, LANES - FOX_HEADS))).astype(BF)
        wg = w[:, OFF_G:].astype(BF)
        qg = _row(jnp.tile(fox_q_g[layer], FOX_HEADS) * (FOX_HEAD_DIM ** -0.5 * LOG2E))
        kg = _row(jnp.tile(fox_k_g[layer], FOX_HEADS))
        mqg = _row(mem_q_g[layer] * MEM_HEAD_DIM ** -0.5)

        mkT, mv = _mem_kv(mem, _row(mem_norm_g[layer]), w_mem_kv[layer].astype(BF), _row(mem_k_g[layer]))
        u, q, kT, v, logf, mem_o = _in_proj(
            x2d, B, S, _row(mix_norm_g[layer]), wa, wf, _row(b_forget[layer], LANES), qg, kg,
            head_mean, mqg, mkT, mv)
        score_bound = (NORM_SLACK * FOX_HEAD_DIM * jnp.max(jnp.abs(qg)) * jnp.max(jnp.abs(kg)))
        fox_o = _fox(score_bound, q, _cumsum(logf, kT), v).reshape(T, FOX_WIDTH)

        merge_args = (x2d, S, u, fox_o, mem_o, _row(mix_norm_g[layer]), wg, pool_w[layer].astype(BF),
                      _row(pool_scale[layer]), w_pool_br[layer].astype(BF), w_fox_br[layer].astype(BF),
                      w_mem_br[layer].astype(BF), w_out[layer].astype(BF), _row(ffn_norm_g[layer]))
        if layer % 2 == 0:
            x2d = _merge_ffn(*merge_args, w_ffn_gu[layer // 2].astype(BF), w_ffn_down[layer // 2].astype(BF))
        else:
            m = layer // 2
            wr = jnp.pad(w_router[m], ((0, 0), (0, LANES - N_EXPERTS)))
            x1, h2, route = _merge_router(*merge_args, wr, _row(b_router[m], LANES))
            pos, pad_start, tile_expert, n_used, n_slots = _dispatch_plan(route, T)
            xs = _dispatch(h2, pos, pad_start, n_slots)
            y = _experts(xs, tile_expert, n_used, w_exp_gu[m].astype(BF), w_exp_down[m].astype(BF))
            x2d = _combine(x1, route, y, pos)
    return x2d.reshape(B, S, D)
```
